```python
import jax, jax.numpy as jnp
from jax import lax
import numpy as np

D_MODEL = 4096
BATCH = 2
SEQ = 8192
DEPTH = 2

HEAD_DIM = 128
NA_HEADS = D_MODEL // HEAD_DIM
NA_WIDTH = NA_HEADS * HEAD_DIM
GRID_W = 64
NA_KH_MAX = 8
NA_KW = 16
DL_GROUPS = ((128, 1), (512, 4), (2048, 16))
N_GROUPS = len(DL_GROUPS)
DL_HEADS = D_MODEL // (2 * HEAD_DIM)
DL_WIDTH = DL_HEADS * HEAD_DIM
ROPE_THETA = 500000.0
ROPE_DIM = HEAD_DIM // 4
RMS_EPS = 1e-6
NEG_INF = -1e30
SCALE = HEAD_DIM ** -0.5

kernel_name = "hybrid_natten_dilated_gated_encoder"


def rms_norm(x, g):
    xf = x.astype(jnp.float32)
    y = xf * lax.rsqrt(jnp.mean(xf * xf, axis=-1, keepdims=True) + RMS_EPS)
    return (y * g.astype(jnp.float32)).astype(x.dtype)


def partial_rope(x, pos):
    half = ROPE_DIM // 2
    inv_freq = jnp.power(ROPE_THETA, -jnp.arange(half, dtype=jnp.float32) * (2.0 / ROPE_DIM))
    ang = pos.astype(jnp.float32)[:, None] * inv_freq[None, :]
    cos = jnp.cos(ang)[None, :, None, :]
    sin = jnp.sin(ang)[None, :, None, :]
    xr = x[..., :ROPE_DIM].astype(jnp.float32)
    x1, x2 = xr[..., :half], xr[..., half:]
    rot = jnp.concatenate([x1 * cos - x2 * sin, x2 * cos + x1 * sin], axis=-1).astype(x.dtype)
    return jnp.concatenate([rot, x[..., ROPE_DIM:]], axis=-1)


def neighborhood_mixer(xn, w_in, rpb, w_out):
    B, S, _ = xn.shape
    rows = S // GRID_W
    kh = min(NA_KH_MAX, rows)
    proj = xn @ w_in
    qkv = proj[..., :3 * NA_WIDTH].reshape(B, rows, GRID_W, 3, NA_HEADS, HEAD_DIM)
    gate = proj[..., 3 * NA_WIDTH:]
    q = qkv[:, :, :, 0].transpose(0, 3, 1, 2, 4)
    k = qkv[:, :, :, 1].transpose(0, 3, 1, 2, 4)
    v = qkv[:, :, :, 2].transpose(0, 3, 1, 2, 4)
    col = jnp.arange(GRID_W)
    cs = jnp.clip(col - NA_KW // 2, 0, GRID_W - NA_KW)
    col_ok = (col[None, :] >= cs[:, None]) & (col[None, :] < cs[:, None] + NA_KW)
    dc = jnp.clip(col[None, :] - col[:, None] + NA_KW - 1, 0, 2 * NA_KW - 2)

    def row_attend(r):
        rs = jnp.clip(r - kh // 2, 0, rows - kh)
        kb = lax.dynamic_slice_in_dim(k, rs, kh, axis=2)
        vb = lax.dynamic_slice_in_dim(v, rs, kh, axis=2)
        qr = lax.dynamic_index_in_dim(q, r, axis=2, keepdims=False)
        dr = rs + jnp.arange(kh) - r + NA_KH_MAX - 1
        bias = rpb[:, dr[None, :, None], dc[:, None, :]].astype(jnp.float32)
        s = jnp.einsum('bhqd,bhrkd->bhqrk', qr, kb).astype(jnp.float32) * SCALE + bias
        s = jnp.where(col_ok[:, None, :], s, NEG_INF)
        p = jax.nn.softmax(s.reshape(B, NA_HEADS, GRID_W, kh * GRID_W), axis=-1).reshape(s.shape)
        return jnp.einsum('bhqrk,bhrkd->bhqd', p.astype(vb.dtype), vb)

    o = lax.map(row_attend, jnp.arange(rows))
    o = o.transpose(1, 0, 3, 2, 4).reshape(B, S, NA_WIDTH)
    return (o * jax.nn.silu(gate)) @ w_out


def banded_window_attention(q, k, v, half):
    *lead, L, hd = q.shape
    qb = half
    nb = -(-L // qb)
    lp = nb * qb
    pad_q = [(0, 0)] * len(lead) + [(0, lp - L), (0, 0)]
    pad_kv = [(0, 0)] * len(lead) + [(qb, lp - L + qb), (0, 0)]
    qblk = jnp.pad(q, pad_q).reshape(*lead, nb, qb, hd)
    kp = jnp.pad(k, pad_kv).reshape(*lead, nb + 2, qb, hd)
    vp = jnp.pad(v, pad_kv).reshape(*lead, nb + 2, qb, hd)
    kblk = jnp.concatenate([kp[..., :-2, :, :], kp[..., 1:-1, :, :], kp[..., 2:, :, :]], axis=-2)
    vblk = jnp.concatenate([vp[..., :-2, :, :], vp[..., 1:-1, :, :], vp[..., 2:, :, :]], axis=-2)
    qpos = jnp.arange(nb)[:, None] * qb + jnp.arange(qb)[None, :]
    kpos = jnp.arange(nb)[:, None] * qb - qb + jnp.arange(3 * qb)[None, :]
    mask = (jnp.abs(kpos[:, None, :] - qpos[:, :, None]) <= half) & ((kpos >= 0) & (kpos < L))[:, None, :]
    s = jnp.einsum('...nqd,...nkd->...nqk', qblk, kblk).astype(jnp.float32) * SCALE
    s = jnp.where(mask, s, NEG_INF)
    m = jnp.max(s, axis=-1, keepdims=True)
    p = jnp.exp(s - m)
    den = jnp.sum(p, axis=-1, keepdims=True)
    o = jnp.einsum('...nqk,...nkd->...nqd', (p / den).astype(v.dtype), vblk)
    lse = (m + jnp.log(den))[..., 0]
    o = o.reshape(*lead, lp, hd)[..., :L, :]
    lse = lse.reshape(*lead, lp)[..., :L]
    return o, lse


def to_strided(t, dil):
    B, S, H, hd = t.shape
    return t.reshape(B, S // dil, dil, H, hd).transpose(0, 3, 2, 1, 4)


def dilated_mixer(xn, w_in, w_out):
    B, S, _ = xn.shape
    proj = xn @ w_in
    qkv = proj[..., :3 * N_GROUPS * DL_WIDTH].reshape(B, S, N_GROUPS, 3, DL_HEADS, HEAD_DIM)
    gate = proj[..., 3 * N_GROUPS * DL_WIDTH:]
    pos = jnp.arange(S)
    outs, lses = [], []
    for g, (window, dil) in enumerate(DL_GROUPS):
        half = window // (2 * dil)
        q = partial_rope(qkv[:, :, g, 0], pos)
        k = partial_rope(qkv[:, :, g, 1], pos)
        v = qkv[:, :, g, 2]
        o, lse = banded_window_attention(to_strided(q, dil), to_strided(k, dil), to_strided(v, dil), half)
        outs.append(o.transpose(0, 3, 2, 1, 4).reshape(B, S, DL_HEADS, HEAD_DIM))
        lses.append(lse.transpose(0, 3, 2, 1).reshape(B, S, DL_HEADS))
    wts = jax.nn.softmax(jnp.stack(lses, axis=0), axis=0)
    o = jnp.einsum('gbsh,gbshd->bshd', wts, jnp.stack(outs, axis=0).astype(jnp.float32))
    y = o.reshape(B, S, DL_WIDTH).astype(xn.dtype) * jax.nn.silu(gate)
    return y @ w_out


def setup_inputs(seed: int = 0) -> dict:
    key = jax.random.key(seed)
    ks = jax.random.split(key, 8)
    n_na = (DEPTH + 1) // 2
    n_dl = DEPTH // 2
    f32 = jnp.float32
    x = jax.random.normal(ks[0], (BATCH, SEQ, D_MODEL), f32)
    norm_w = 1.0 + 0.02 * jax.random.normal(ks[1], (DEPTH, D_MODEL), f32)
    final_norm_w = 1.0 + 0.02 * jax.random.normal(ks[2], (D_MODEL,), f32)
    na_w_in = jax.random.normal(ks[3], (n_na, D_MODEL, 4 * NA_WIDTH), f32) * (D_MODEL ** -0.5)
    na_rpb = 0.5 * jax.random.normal(ks[4], (n_na, NA_HEADS, 2 * NA_KH_MAX - 1, 2 * NA_KW - 1), f32)
    na_w_out = jax.random.normal(ks[5], (n_na, NA_WIDTH, D_MODEL), f32) * (NA_WIDTH ** -0.5)
    dl_w_in = jax.random.normal(ks[6], (n_dl, D_MODEL, (3 * N_GROUPS + 1) * DL_WIDTH), f32) * (D_MODEL ** -0.5)
    dl_w_out = jax.random.normal(ks[7], (n_dl, DL_WIDTH, D_MODEL), f32) * (DL_WIDTH ** -0.5)
    return {"x": x, "norm_w": norm_w, "final_norm_w": final_norm_w,
            "na_w_in": na_w_in, "na_rpb": na_rpb, "na_w_out": na_w_out,
            "dl_w_in": dl_w_in, "dl_w_out": dl_w_out}


def reference(x, norm_w, final_norm_w, na_w_in, na_rpb, na_w_out, dl_w_in, dl_w_out):
    for i in range(DEPTH):
        xn = rms_norm(x, norm_w[i])
        j = i // 2
        if i % 2 == 0:
            x = x + neighborhood_mixer(xn, na_w_in[j], na_rpb[j], na_w_out[j])
        else:
            x = x + dilated_mixer(xn, dl_w_in[j], dl_w_out[j])
    return rms_norm(x, final_norm_w)
```

```python
import functools

import jax
import jax.numpy as jnp
import numpy as np
from jax import lax
from jax.experimental import pallas as pl
from jax.experimental.pallas import tpu as pltpu

HEAD_DIM = 128
GRID_W = 64
NA_KH = 8
NA_KW = 16
DL_GROUPS = ((128, 1), (512, 4), (2048, 16))
ROPE_THETA = 500000.0
ROPE_DIM = HEAD_DIM // 4
RMS_EPS = 1e-6
NEG_INF = -1e30
SCALE = HEAD_DIM ** -0.5

LANES = 128
VMEM_LIMIT_BYTES = 56 * 1024 * 1024

BF16 = jnp.bfloat16
F32 = jnp.float32


def _params(*semantics):
    return pltpu.CompilerParams(dimension_semantics=semantics, vmem_limit_bytes=VMEM_LIMIT_BYTES)


def _rmsnorm_kernel(x_ref, g_ref, *rest, dils):
    out_refs, y_ref = rest[:-1], rest[-1]
    x = x_ref[0]
    y = x * lax.rsqrt(jnp.mean(x * x, axis=-1, keepdims=True) + RMS_EPS)
    y = y * g_ref[...]
    tm, D = x.shape
    n_chunks = D // LANES
    if any(d > 1 for d in dils):
        for c in range(n_chunks):
            y_ref[c] = y[:, c * LANES:(c + 1) * LANES]
    for d, o_ref in zip(dils, out_refs):
        if d == 1:
            o_ref[0, 0] = y.astype(o_ref.dtype)
        else:
            for c in range(n_chunks):
                for r in range(d):
                    o_ref[0, r, :, c * LANES:(c + 1) * LANES] = (
                        y_ref[c, pl.ds(r, tm // d, stride=d), :].astype(o_ref.dtype))


def _rmsnorm(x, g, dils, tm):
    B, S, D = x.shape
    out_shape = [jax.ShapeDtypeStruct((B, d, S // d, D), BF16) for d in dils]
    out_specs = [pl.BlockSpec((1, d, tm // d, D), lambda b, i: (b, 0, i, 0)) for d in dils]
    return pl.pallas_call(
        functools.partial(_rmsnorm_kernel, dils=dils),
        grid=(B, S // tm),
        in_specs=[pl.BlockSpec((1, tm, D), lambda b, i: (b, i, 0)),
                  pl.BlockSpec((1, D), lambda b, i: (0, 0))],
        out_specs=out_specs,
        out_shape=out_shape,
        scratch_shapes=[pltpu.VMEM((D // LANES, tm, LANES), F32)],
        compiler_params=_params("parallel", "parallel"),
        name="rmsnorm",
    )(x, g.reshape(1, D))


def _rope(x, cos, sin_hi, sin_lo):
    half = ROPE_DIM // 2
    return (x * cos + pltpu.roll(x, half, 1) * sin_hi + pltpu.roll(x, LANES - half, 1) * sin_lo)


def _matmul_kernel(a_ref, w_ref, *rest, n_rope_tiles, has_res):
    rest = list(rest)
    o_ref = rest.pop()
    acc = jnp.dot(a_ref[...], w_ref[...], preferred_element_type=F32)
    if has_res:
        o_ref[...] = (rest[0][...] + acc).astype(o_ref.dtype)
    elif n_rope_tiles:
        cos_ref, shi_ref, slo_ref = rest
        j = pl.program_id(1)

        @pl.when(j < n_rope_tiles)
        def _():
            cos, shi, slo = cos_ref[...], shi_ref[...], slo_ref[...]
            for c in range(acc.shape[1] // LANES):
                sl = slice(c * LANES, (c + 1) * LANES)
                o_ref[:, sl] = _rope(acc[:, sl], cos, shi, slo).astype(o_ref.dtype)

        @pl.when(j >= n_rope_tiles)
        def _():
            o_ref[...] = acc.astype(o_ref.dtype)
    else:
        o_ref[...] = acc.astype(o_ref.dtype)


def _matmul(a, w, *, n_cols, col_off, out_dtype, tm, tn, res=None, rope=None, rope_cols=0, name):
    M, K = a.shape
    assert M % tm == 0 and n_cols % tn == 0 and col_off % tn == 0 and rope_cols % tn == 0
    joff = col_off // tn
    in_specs = [pl.BlockSpec((tm, K), lambda i, j: (i, 0)),
                pl.BlockSpec((K, tn), lambda i, j: (0, j + joff))]
    args = [a, w]
    if res is not None:
        in_specs.append(pl.BlockSpec((tm, tn), lambda i, j: (i, j)))
        args.append(res)
    if rope is not None:
        nper = rope[0].shape[0] // tm
        assert rope[0].shape[0] % tm == 0
        for t in rope:
            in_specs.append(pl.BlockSpec((tm, LANES), lambda i, j: (i % nper, 0)))
            args.append(t)
    return pl.pallas_call(
        functools.partial(_matmul_kernel, n_rope_tiles=rope_cols // tn if rope is not None else 0,
                          has_res=res is not None),
        grid=(M // tm, n_cols // tn),
        in_specs=in_specs,
        out_specs=pl.BlockSpec((tm, tn), lambda i, j: (i, j)),
        out_shape=jax.ShapeDtypeStruct((M, n_cols), out_dtype),
        compiler_params=_params("parallel", "arbitrary"),
        name=name,
    )(*args)


NA_QROWS = 8
NA_KROWS = 16


def _na_block_tables(rows):
    nblk = rows // NA_QROWS
    kb0s, vids, variants = [], [], []
    for i in range(nblk):
        kb0 = min(max(NA_QROWS * i - NA_KH // 2, 0), rows - NA_KROWS)
        tab = []
        for a in range(NA_QROWS):
            r = NA_QROWS * i + a
            rs = min(max(r - NA_KH // 2, 0), rows - NA_KH)
            for kl in range(NA_KROWS):
                kr = kb0 + kl
                tab.append(kr - r + NA_KH - 1 if rs <= kr < rs + NA_KH else None)
        tab = tuple(tab)
        if tab not in variants:
            variants.append(tab)
        kb0s.append(kb0)
        vids.append(variants.index(tab))
    return kb0s, vids, variants


def _na_kernel(rpb_ref, kb0_ref, vid_ref, q_ref, k_ref, v_ref, g_ref, o_ref, tile_ref, bias_ref,
               *, variants, nblk):
    h = pl.program_id(0)
    W = GRID_W
    n_dr, n_dc = 2 * NA_KH - 1, 2 * NA_KW - 1

    @pl.when(pl.program_id(1) == 0)
    def _build_bias():
        qc = lax.broadcasted_iota(jnp.int32, (W, LANES), 0)
        lane = lax.broadcasted_iota(jnp.int32, (W, LANES), 1)
        kc = lane & (W - 1)
        cs = jnp.clip(qc - NA_KW // 2, 0, W - NA_KW)
        col_ok = (kc >= cs) & (kc < cs + NA_KW)
        dc = jnp.clip(kc - qc + NA_KW - 1, 0, n_dc - 1)

        def build_tile(dr, carry):
            t = jnp.zeros((W, LANES), F32)
            base = (h * n_dr + dr) * n_dc
            for j in range(n_dc):
                t = jnp.where(dc == j, rpb_ref[base + j], t)
            tile_ref[dr] = jnp.where(col_ok, t, NEG_INF)
            return carry

        lax.fori_loop(0, n_dr, build_tile, 0)
        neg = jnp.full((W, LANES), NEG_INF, F32)
        for vi, tab in enumerate(variants):
            for a in range(NA_QROWS):
                for p in range(NA_KROWS // 2):
                    d0 = tab[a * NA_KROWS + 2 * p]
                    d1 = tab[a * NA_KROWS + 2 * p + 1]
                    t0 = neg if d0 is None else tile_ref[d0]
                    t1 = neg if d1 is None else tile_ref[d1]
                    bias_ref[vi, a * W:(a + 1) * W, p * LANES:(p + 1) * LANES] = jnp.where(lane < W, t0, t1)

    nq, nk = NA_QROWS * W, NA_KROWS * W

    def block(i, carry):
        q0 = pl.multiple_of(i * nq, nq)
        k0 = pl.multiple_of(kb0_ref[i] * W, 2 * LANES)
        q = q_ref[0, pl.ds(q0, nq), :]
        k = k_ref[0, pl.ds(k0, nk), :]
        v = v_ref[0, pl.ds(k0, nk), :]
        s = lax.dot_general(q, k, (((1,), (1,)), ((), ())), preferred_element_type=F32)
        s = s * SCALE + bias_ref[vid_ref[i]]
        m = jnp.max(s, axis=-1, keepdims=True)
        p = jnp.exp(s - m)
        den = jnp.sum(p, axis=-1, keepdims=True)
        o = jnp.dot(p.astype(BF16), v, preferred_element_type=F32) / den
        g = g_ref[0, pl.ds(q0, nq), :]
        o_ref[0, pl.ds(q0, nq), :] = (o * (g / (1.0 + jnp.exp(-g)))).astype(o_ref.dtype)
        return carry

    lax.fori_loop(0, nblk, block, 0)


def _na_attention(qkv, gate, rpb, *, heads):
    B, S, _ = qkv.shape
    rows = S // GRID_W
    assert rows % NA_QROWS == 0 and rows >= NA_KROWS
    kb0s, vids, variants = _na_block_tables(rows)
    nblk = rows // NA_QROWS
    smem = pl.BlockSpec(memory_space=pltpu.SMEM)
    tok = lambda off: pl.BlockSpec((1, S, HEAD_DIM), lambda h, b: (b, 0, h + off))
    return pl.pallas_call(
        functools.partial(_na_kernel, variants=variants, nblk=nblk),
        grid=(heads, B),
        in_specs=[smem, smem, smem, tok(0), tok(heads), tok(2 * heads), tok(0)],
        out_specs=tok(0),
        out_shape=jax.ShapeDtypeStruct((B, S, heads * HEAD_DIM), BF16),
        scratch_shapes=[pltpu.VMEM((2 * NA_KH - 1, GRID_W, LANES), F32),
                        pltpu.VMEM((len(variants), NA_QROWS * GRID_W, NA_KROWS * GRID_W), F32)],
        compiler_params=_params("arbitrary", "arbitrary"),
        name="na_attention",
    )(rpb.reshape(-1), jnp.asarray(kb0s, jnp.int32), jnp.asarray(vids, jnp.int32), qkv, qkv, qkv, gate)


DL_QB = 256


def _dilated_kernel(q_ref, k_ref, v_ref, o_ref, lse_ref, *, dil, half, L):
    h = pl.program_id(1)
    qb = min(DL_QB, L - 2 * half)
    kb = qb + 2 * half
    nqb = L // qb

    @pl.when(h == 0)
    def _():
        lse_ref[...] = jnp.zeros_like(lse_ref)

    rel = (lax.broadcasted_iota(jnp.int32, (qb, kb), 1) - lax.broadcasted_iota(jnp.int32, (qb, kb), 0))
    lane = lax.broadcasted_iota(jnp.int32, (qb, LANES), 1)

    def block(n, carry):
        r = n // nqb
        q0 = pl.multiple_of((n % nqb) * qb, qb)
        k0 = pl.multiple_of(jnp.clip(q0 - half, 0, L - kb), half)
        q = q_ref[0, r, pl.ds(q0, qb), :]
        k = k_ref[0, r, pl.ds(k0, kb), :]
        v = v_ref[0, r, pl.ds(k0, kb), :]
        s = lax.dot_general(q, k, (((1,), (1,)), ((), ())), preferred_element_type=F32) * SCALE
        s = jnp.where(jnp.abs(rel + (k0 - q0)) <= half, s, NEG_INF)
        m = jnp.max(s, axis=-1, keepdims=True)
        p = jnp.exp(s - m)
        den = jnp.sum(p, axis=-1, keepdims=True)
        o = jnp.dot(p.astype(BF16), v, preferred_element_type=F32) / den
        lse = m + jnp.log(den)
        if dil == 1:
            rows = pl.ds(q0, qb)
        else:
            rows = pl.ds(q0 * dil + r, qb, stride=dil)
        o_ref[0, rows, :] = o
        lse_ref[0, rows, :] = jnp.where(lane == h, lse, lse_ref[0, rows, :])
        return carry

    lax.fori_loop(0, dil * nqb, block, 0)


def _dilated_attention(qkv, *, B, dil, half, heads):
    L = qkv.shape[2]
    S = L * dil
    assert L % min(DL_QB, L - 2 * half) == 0 and heads <= LANES
    blk = lambda off: pl.BlockSpec((1, dil, L, HEAD_DIM), lambda b, h: (b, 0, 0, h + off))
    return pl.pallas_call(
        functools.partial(_dilated_kernel, dil=dil, half=half, L=L),
        grid=(B, heads),
        in_specs=[blk(0), blk(heads), blk(2 * heads)],
        out_specs=[pl.BlockSpec((1, S, HEAD_DIM), lambda b, h: (b, 0, h)),
                   pl.BlockSpec((1, S, LANES), lambda b, h: (b, 0, 0))],
        out_shape=[jax.ShapeDtypeStruct((B, S, heads * HEAD_DIM), F32),
                   jax.ShapeDtypeStruct((B, S, LANES), F32)],
        compiler_params=_params("arbitrary", "arbitrary"),
        name=f"dilated_attention_d{dil}",
    )(qkv, qkv, qkv)


def _mix_out_kernel(*refs, n_groups, heads):
    o_refs = refs[:n_groups]
    l_refs = refs[n_groups:2 * n_groups]
    g_ref, w_ref, x_ref, nw_ref, out_ref, y_ref = refs[2 * n_groups:]
    lses = [r[...] for r in l_refs]
    mx = functools.reduce(jnp.maximum, lses)
    es = [jnp.exp(l - mx) for l in lses]
    inv = 1.0 / functools.reduce(lambda a, b: a + b, es)
    wts = [e * inv for e in es]
    for h in range(heads):
        sl = slice(h * HEAD_DIM, (h + 1) * HEAD_DIM)
        o = None
        for wt, o_ref in zip(wts, o_refs):
            t = wt[:, h:h + 1] * o_ref[:, sl]
            o = t if o is None else o + t
        g = g_ref[:, sl]
        y_ref[:, sl] = (o * (g / (1.0 + jnp.exp(-g)))).astype(y_ref.dtype)
    x = x_ref[...] + jnp.dot(y_ref[...], w_ref[...], preferred_element_type=F32)
    y = x * lax.rsqrt(jnp.mean(x * x, axis=-1, keepdims=True) + RMS_EPS)
    out_ref[...] = y * nw_ref[...]


def _mix_out(os_, lses, gate, w, x, norm_w, *, heads, tm):
    M, D = x.shape
    width = heads * HEAD_DIM
    n = len(os_)
    row = lambda c: pl.BlockSpec((tm, c), lambda i: (i, 0))
    return pl.pallas_call(
        functools.partial(_mix_out_kernel, n_groups=n, heads=heads),
        grid=(M // tm,),
        in_specs=[row(width)] * n + [row(LANES)] * n + [
            row(width),
            pl.BlockSpec((width, D), lambda i: (0, 0), pipeline_mode=pl.Buffered(1)),
            row(D),
            pl.BlockSpec((1, D), lambda i: (0, 0))],
        out_specs=row(D),
        out_shape=jax.ShapeDtypeStruct((M, D), F32),
        scratch_shapes=[pltpu.VMEM((tm, width), BF16)],
        compiler_params=_params("parallel"),
        name="mix_out_norm",
    )(*os_, *lses, gate, w, x, norm_w.reshape(1, D))


def _rope_tables(S, dil):
    half = ROPE_DIM // 2
    pos = jnp.arange(S).reshape(S // dil, dil).T.reshape(S)
    inv_freq = jnp.power(ROPE_THETA, -jnp.arange(half, dtype=F32) * (2.0 / ROPE_DIM))
    ang = pos.astype(F32)[:, None] * inv_freq[None, :]
    cos, sin = jnp.cos(ang), jnp.sin(ang)
    zeros = jnp.zeros((S, LANES - ROPE_DIM), F32)
    z16 = jnp.zeros((S, half), F32)
    cos_t = jnp.concatenate([cos, cos, 1.0 + zeros], axis=1)
    sin_hi = jnp.concatenate([z16, sin, zeros], axis=1)
    sin_lo = jnp.concatenate([-sin, z16, zeros], axis=1)
    return cos_t, sin_hi, sin_lo


def _forward(x, norm_w, final_norm_w, na_w_in, na_rpb, na_w_out, dl_w_in, dl_w_out, *,
             tm_norm=256, tm=512, tn=1024, tm_out=256):
    B, S, D = x.shape
    M = B * S
    assert norm_w.shape[0] == 2 and na_w_in.shape[0] == 1 and dl_w_in.shape[0] == 1
    na_width = na_w_out.shape[1]
    na_heads = na_width // HEAD_DIM
    dl_width = dl_w_out.shape[1]
    dl_heads = dl_width // HEAD_DIM
    n_groups = len(DL_GROUPS)

    w_in0 = na_w_in[0].astype(BF16)
    w_out0 = na_w_out[0].astype(BF16)
    (xn0,) = _rmsnorm(x, norm_w[0], (1,), tm_norm)
    xn0 = xn0.reshape(M, D)
    qkv0 = _matmul(xn0, w_in0, n_cols=3 * na_width, col_off=0, out_dtype=BF16, tm=tm, tn=tn, name="proj0_qkv")
    gate0 = _matmul(xn0, w_in0, n_cols=na_width, col_off=3 * na_width, out_dtype=F32, tm=tm, tn=tn,
                    name="proj0_gate")
    y0 = _na_attention(qkv0.reshape(B, S, 3 * na_width), gate0.reshape(B, S, na_width), na_rpb[0],
                       heads=na_heads)
    x1 = _matmul(y0.reshape(M, na_width), w_out0, n_cols=D, col_off=0, out_dtype=F32, tm=tm, tn=tn,
                 res=x.reshape(M, D), name="out0")

    w_in1 = dl_w_in[0].astype(BF16)
    w_out1 = dl_w_out[0].astype(BF16)
    dils = tuple(sorted({d for _, d in DL_GROUPS}))
    xn1 = dict(zip(dils, _rmsnorm(x1.reshape(B, S, D), norm_w[1], dils, tm_norm)))
    os_, lses = [], []
    for gi, (window, dil) in enumerate(DL_GROUPS):
        half = window // (2 * dil)
        qkv = _matmul(xn1[dil].reshape(M, D), w_in1, n_cols=3 * dl_width, col_off=gi * 3 * dl_width,
                      out_dtype=BF16, tm=min(tm, S // dil), tn=tn, rope=_rope_tables(S, dil),
                      rope_cols=2 * dl_width, name=f"proj1_g{gi}")
        o, lse = _dilated_attention(qkv.reshape(B, dil, S // dil, 3 * dl_width), B=B, dil=dil, half=half,
                                    heads=dl_heads)
        os_.append(o.reshape(M, dl_width))
        lses.append(lse.reshape(M, LANES))
    gate1 = _matmul(xn1[1].reshape(M, D), w_in1, n_cols=dl_width, col_off=3 * n_groups * dl_width,
                    out_dtype=F32, tm=tm, tn=tn, name="proj1_gate")
    out = _mix_out(os_, lses, gate1, w_out1, x1, final_norm_w, heads=dl_heads, tm=tm_out)
    return out.reshape(B, S, D)


def kernel(x, norm_w, final_norm_w, na_w_in, na_rpb, na_w_out, dl_w_in, dl_w_out):
    return _forward(x, norm_w, final_norm_w, na_w_in, na_rpb, na_w_out, dl_w_in, dl_w_out)
```

```python
import functools
import math

import jax
import jax.numpy as jnp
from jax import lax
from jax.experimental import pallas as pl
from jax.experimental.pallas import tpu as pltpu

HEAD_DIM = 128
GRID_W = 64
NA_KH = 8
NA_KW = 16
DL_GROUPS = ((128, 1), (512, 4), (2048, 16))
ROPE_THETA = 500000.0
ROPE_DIM = HEAD_DIM // 4
RMS_EPS = 1e-6
NEG_INF = -1e30
SCALE = HEAD_DIM ** -0.5
LOG2E = math.log2(math.e)
LN2 = math.log(2.0)
Q_SCALE = SCALE * LOG2E

LANES = 128
MXU_COLS = 256
VMEM_LIMIT_BYTES = 56 * 1024 * 1024

BF16 = jnp.bfloat16
F32 = jnp.float32


def _params(*semantics):
    return pltpu.CompilerParams(dimension_semantics=semantics, vmem_limit_bytes=VMEM_LIMIT_BYTES)


def _softmax_pv(s, v):
    m = jnp.max(s, axis=-1, keepdims=True)
    p = jnp.exp2(s - m).astype(BF16)
    v_aug = jnp.concatenate([v, jnp.ones_like(v)], axis=1)
    pv = jnp.dot(p, v_aug, preferred_element_type=F32)
    den = pv[:, HEAD_DIM:]
    return pv[:, :HEAD_DIM] / den, m, den


def _rmsnorm_kernel(x_ref, g_ref, *rest, dils):
    out_refs, y_ref = rest[:-1], rest[-1]
    x = x_ref[0]
    y = x * lax.rsqrt(jnp.mean(x * x, axis=-1, keepdims=True) + RMS_EPS)
    y = y * g_ref[...]
    tm, D = x.shape
    n_chunks = D // LANES
    if any(d > 1 for d in dils):
        for c in range(n_chunks):
            y_ref[c] = y[:, c * LANES:(c + 1) * LANES]
    for d, o_ref in zip(dils, out_refs):
        if d == 1:
            o_ref[0, 0] = y.astype(o_ref.dtype)
        else:
            for c in range(n_chunks):
                for r in range(d):
                    o_ref[0, r, :, c * LANES:(c + 1) * LANES] = (
                        y_ref[c, pl.ds(r, tm // d, stride=d), :].astype(o_ref.dtype))


def _rmsnorm(x, g, dils, tm):
    B, S, D = x.shape
    out_shape = [jax.ShapeDtypeStruct((B, d, S // d, D), BF16) for d in dils]
    out_specs = [pl.BlockSpec((1, d, tm // d, D), lambda b, i: (b, 0, i, 0)) for d in dils]
    return pl.pallas_call(
        functools.partial(_rmsnorm_kernel, dils=dils),
        grid=(B, S // tm),
        in_specs=[pl.BlockSpec((1, tm, D), lambda b, i: (b, i, 0)),
                  pl.BlockSpec((1, D), lambda b, i: (0, 0))],
        out_specs=out_specs,
        out_shape=out_shape,
        scratch_shapes=[pltpu.VMEM((D // LANES, tm, LANES), F32)],
        compiler_params=_params("parallel", "parallel"),
        name="rmsnorm",
    )(x, g.reshape(1, D))


def _matmul_kernel(a_ref, w_ref, *rest, mode, n_q_tiles):
    rest = list(rest)
    o_ref = rest.pop()
    tn = o_ref.shape[1]
    j = pl.program_id(1)
    if mode == "plain" and n_q_tiles:
        fac = jnp.where(j < n_q_tiles, Q_SCALE, 1.0).astype(F32)
    if mode == "rope":
        cos_ref, shi_ref, slo_ref = rest
        half = ROPE_DIM // 2
    for c in range(tn // MXU_COLS):
        cols = slice(c * MXU_COLS, (c + 1) * MXU_COLS)
        acc = jnp.dot(a_ref[...], w_ref[:, cols], preferred_element_type=F32)
        if mode == "res":
            acc = rest[0][:, cols] + acc
        elif mode == "rope":
            heads = []
            for hh in range(MXU_COLS // LANES):
                x = acc[:, hh * LANES:(hh + 1) * LANES]
                heads.append(x * cos_ref[0] + pltpu.roll(x, half, 1) * shi_ref[0]
                             + pltpu.roll(x, LANES - half, 1) * slo_ref[0])
            acc = jnp.concatenate(heads, axis=1)
        elif n_q_tiles:
            acc = acc * fac
        o_ref[:, cols] = acc.astype(o_ref.dtype)


def _matmul(a, w, *, n_cols, col_off, out_dtype, tm, tn, res=None, rope=None, n_q_cols=0, name):
    M, K = a.shape
    assert M % tm == 0 and n_cols % tn == 0 and col_off % tn == 0 and n_q_cols % tn == 0
    assert tn % MXU_COLS == 0
    joff = col_off // tn
    n_q_tiles = n_q_cols // tn
    in_specs = [pl.BlockSpec((tm, K), lambda i, j: (i, 0)),
                pl.BlockSpec((K, tn), lambda i, j: (0, j + joff))]
    args = [a, w]
    mode = "plain"
    if res is not None:
        mode = "res"
        in_specs.append(pl.BlockSpec((tm, tn), lambda i, j: (i, j)))
        args.append(res)
    if rope is not None:
        mode = "rope"
        kinds, P, _ = rope[0].shape
        assert P % tm == 0 and n_q_tiles > 0
        nper = P // tm
        for t in rope:
            in_specs.append(pl.BlockSpec(
                (1, tm, LANES), lambda i, j: (jnp.minimum(j // n_q_tiles, kinds - 1), i % nper, 0)))
            args.append(t)
    return pl.pallas_call(
        functools.partial(_matmul_kernel, mode=mode, n_q_tiles=n_q_tiles),
        grid=(M // tm, n_cols // tn),
        in_specs=in_specs,
        out_specs=pl.BlockSpec((tm, tn), lambda i, j: (i, j)),
        out_shape=jax.ShapeDtypeStruct((M, n_cols), out_dtype),
        compiler_params=_params("parallel", "arbitrary"),
        name=name,
    )(*args)


NA_QROWS = 8
NA_KROWS = 16


def _na_block_tables(rows):
    nblk = rows // NA_QROWS
    kb0s, vids, variants = [], [], []
    for i in range(nblk):
        kb0 = min(max(NA_QROWS * i - NA_KH // 2, 0), rows - NA_KROWS)
        tab = []
        for a in range(NA_QROWS):
            r = NA_QROWS * i + a
            rs = min(max(r - NA_KH // 2, 0), rows - NA_KH)
            for kl in range(NA_KROWS):
                kr = kb0 + kl
                tab.append(kr - r + NA_KH - 1 if rs <= kr < rs + NA_KH else None)
        tab = tuple(tab)
        if tab not in variants:
            variants.append(tab)
        kb0s.append(kb0)
        vids.append(variants.index(tab))
    return kb0s, vids, variants


def _na_kernel(rpb_ref, kb0_ref, vid_ref, q_ref, k_ref, v_ref, g_ref, o_ref, tile_ref, bias_ref,
               *, variants, nblk):
    h = pl.program_id(0)
    W = GRID_W
    n_dr, n_dc = 2 * NA_KH - 1, 2 * NA_KW - 1

    @pl.when(pl.program_id(1) == 0)
    def _build_bias():
        qc = lax.broadcasted_iota(jnp.int32, (W, LANES), 0)
        lane = lax.broadcasted_iota(jnp.int32, (W, LANES), 1)
        kc = lane & (W - 1)
        cs = jnp.clip(qc - NA_KW // 2, 0, W - NA_KW)
        col_ok = (kc >= cs) & (kc < cs + NA_KW)
        dc = jnp.clip(kc - qc + NA_KW - 1, 0, n_dc - 1)

        def build_tile(dr, carry):
            t = jnp.zeros((W, LANES), F32)
            base = (h * n_dr + dr) * n_dc
            for j in range(n_dc):
                t = jnp.where(dc == j, rpb_ref[base + j] * LOG2E, t)
            tile_ref[dr] = jnp.where(col_ok, t, NEG_INF)
            return carry

        lax.fori_loop(0, n_dr, build_tile, 0)
        neg = jnp.full((W, LANES), NEG_INF, F32)
        for vi, tab in enumerate(variants):
            for a in range(NA_QROWS):
                for p in range(NA_KROWS // 2):
                    d0 = tab[a * NA_KROWS + 2 * p]
                    d1 = tab[a * NA_KROWS + 2 * p + 1]
                    t0 = neg if d0 is None else tile_ref[d0]
                    t1 = neg if d1 is None else tile_ref[d1]
                    bias_ref[vi, a * W:(a + 1) * W, p * LANES:(p + 1) * LANES] = jnp.where(lane < W, t0, t1)

    nq, nk = NA_QROWS * W, NA_KROWS * W

    def block(i, carry):
        q0 = pl.multiple_of(i * nq, nq)
        k0 = pl.multiple_of(kb0_ref[i] * W, 2 * LANES)
        q = q_ref[0, pl.ds(q0, nq), :]
        k = k_ref[0, pl.ds(k0, nk), :]
        v = v_ref[0, pl.ds(k0, nk), :]
        s = lax.dot_general(q, k, (((1,), (1,)), ((), ())), preferred_element_type=F32)
        o, _, _ = _softmax_pv(s + bias_ref[vid_ref[i]], v)
        g = g_ref[0, pl.ds(q0, nq), :]
        o_ref[0, pl.ds(q0, nq), :] = (o * (g / (1.0 + jnp.exp(-g)))).astype(o_ref.dtype)
        return carry

    lax.fori_loop(0, nblk, block, 0, unroll=2)


def _na_attention(qkv, gate, rpb, *, heads):
    B, S, _ = qkv.shape
    rows = S // GRID_W
    assert rows % (2 * NA_QROWS) == 0 and rows >= NA_KROWS
    kb0s, vids, variants = _na_block_tables(rows)
    nblk = rows // NA_QROWS
    smem = pl.BlockSpec(memory_space=pltpu.SMEM)
    tok = lambda off: pl.BlockSpec((1, S, HEAD_DIM), lambda h, b: (b, 0, h + off))
    return pl.pallas_call(
        functools.partial(_na_kernel, variants=variants, nblk=nblk),
        grid=(heads, B),
        in_specs=[smem, smem, smem, tok(0), tok(heads), tok(2 * heads), tok(0)],
        out_specs=tok(0),
        out_shape=jax.ShapeDtypeStruct((B, S, heads * HEAD_DIM), BF16),
        scratch_shapes=[pltpu.VMEM((2 * NA_KH - 1, GRID_W, LANES), F32),
                        pltpu.VMEM((len(variants), NA_QROWS * GRID_W, NA_KROWS * GRID_W), F32)],
        compiler_params=_params("arbitrary", "arbitrary"),
        name="na_attention",
    )(rpb.reshape(-1), jnp.asarray(kb0s, jnp.int32), jnp.asarray(vids, jnp.int32), qkv, qkv, qkv, gate)


DL_QB = 256


def _dilated_kernel(q_ref, k_ref, v_ref, o_ref, lse_ref, mask_ref, *, dil, half, L):
    h = pl.program_id(1)
    qb = min(DL_QB, L - 2 * half)
    kb = qb + 2 * half
    nqb = L // qb

    @pl.when(h == 0)
    def _():
        lse_ref[...] = jnp.zeros_like(lse_ref)

    rel = (lax.broadcasted_iota(jnp.int32, (qb, kb), 1) - lax.broadcasted_iota(jnp.int32, (qb, kb), 0))
    for vi in range(3):
        mask_ref[vi] = jnp.where(jnp.abs(rel - vi * half) <= half, 0.0, NEG_INF)
    lane = lax.broadcasted_iota(jnp.int32, (qb, LANES), 1)

    def block(n, carry):
        r = n // nqb
        q0 = pl.multiple_of((n % nqb) * qb, qb)
        k0 = pl.multiple_of(jnp.clip(q0 - half, 0, L - kb), half)
        q = q_ref[0, r, pl.ds(q0, qb), :]
        k = k_ref[0, r, pl.ds(k0, kb), :]
        v = v_ref[0, r, pl.ds(k0, kb), :]
        s = lax.dot_general(q, k, (((1,), (1,)), ((), ())), preferred_element_type=F32)
        o, m, den = _softmax_pv(s + mask_ref[(q0 - k0) // half], v)
        lse = m * LN2 + jnp.log(den)
        if dil == 1:
            rows = pl.ds(q0, qb)
        else:
            rows = pl.ds(q0 * dil + r, qb, stride=dil)
        o_ref[0, rows, :] = o
        lse_ref[0, rows, :] = jnp.where(lane == h, lse, lse_ref[0, rows, :])
        return carry

    n_blocks = dil * nqb
    lax.fori_loop(0, n_blocks, block, 0, unroll=2 if n_blocks % 2 == 0 else 1)


def _dilated_attention(qkv, *, B, dil, half, heads):
    L = qkv.shape[2]
    S = L * dil
    qb = min(DL_QB, L - 2 * half)
    assert L % qb == 0 and heads <= LANES
    blk = lambda off: pl.BlockSpec((1, dil, L, HEAD_DIM), lambda b, h: (b, 0, 0, h + off))
    return pl.pallas_call(
        functools.partial(_dilated_kernel, dil=dil, half=half, L=L),
        grid=(B, heads),
        in_specs=[blk(0), blk(heads), blk(2 * heads)],
        out_specs=[pl.BlockSpec((1, S, HEAD_DIM), lambda b, h: (b, 0, h)),
                   pl.BlockSpec((1, S, LANES), lambda b, h: (b, 0, 0))],
        out_shape=[jax.ShapeDtypeStruct((B, S, heads * HEAD_DIM), F32),
                   jax.ShapeDtypeStruct((B, S, LANES), F32)],
        scratch_shapes=[pltpu.VMEM((3, qb, qb + 2 * half), F32)],
        compiler_params=_params("arbitrary", "arbitrary"),
        name=f"dilated_attention_d{dil}",
    )(qkv, qkv, qkv)


def _mix_out_kernel(*refs, n_groups, heads):
    o_refs = refs[:n_groups]
    l_refs = refs[n_groups:2 * n_groups]
    g_ref, w_ref, x_ref, nw_ref, out_ref, y_ref = refs[2 * n_groups:]
    lses = [r[...] for r in l_refs]
    mx = functools.reduce(jnp.maximum, lses)
    es = [jnp.exp(l - mx) for l in lses]
    inv = 1.0 / functools.reduce(lambda a, b: a + b, es)
    wts = [e * inv for e in es]
    for h in range(heads):
        sl = slice(h * HEAD_DIM, (h + 1) * HEAD_DIM)
        o = None
        for wt, o_ref in zip(wts, o_refs):
            t = wt[:, h:h + 1] * o_ref[:, sl]
            o = t if o is None else o + t
        g = g_ref[:, sl]
        y_ref[:, sl] = (o * (g / (1.0 + jnp.exp(-g)))).astype(y_ref.dtype)
    x = x_ref[...] + jnp.dot(y_ref[...], w_ref[...], preferred_element_type=F32)
    y = x * lax.rsqrt(jnp.mean(x * x, axis=-1, keepdims=True) + RMS_EPS)
    out_ref[...] = y * nw_ref[...]


def _mix_out(os_, lses, gate, w, x, norm_w, *, heads, tm):
    M, D = x.shape
    width = heads * HEAD_DIM
    n = len(os_)
    row = lambda c: pl.BlockSpec((tm, c), lambda i: (i, 0))
    return pl.pallas_call(
        functools.partial(_mix_out_kernel, n_groups=n, heads=heads),
        grid=(M // tm,),
        in_specs=[row(width)] * n + [row(LANES)] * n + [
            row(width),
            pl.BlockSpec((width, D), lambda i: (0, 0), pipeline_mode=pl.Buffered(1)),
            row(D),
            pl.BlockSpec((1, D), lambda i: (0, 0))],
        out_specs=row(D),
        out_shape=jax.ShapeDtypeStruct((M, D), F32),
        scratch_shapes=[pltpu.VMEM((tm, width), BF16)],
        compiler_params=_params("parallel"),
        name="mix_out_norm",
    )(*os_, *lses, gate, w, x, norm_w.reshape(1, D))


def _rope_tables(S, dil):
    half = ROPE_DIM // 2
    pos = jnp.arange(S).reshape(S // dil, dil).T.reshape(S)
    inv_freq = jnp.power(ROPE_THETA, -jnp.arange(half, dtype=F32) * (2.0 / ROPE_DIM))
    ang = pos.astype(F32)[:, None] * inv_freq[None, :]
    cos, sin = jnp.cos(ang), jnp.sin(ang)
    zeros = jnp.zeros((S, LANES - ROPE_DIM), F32)
    z16 = jnp.zeros((S, half), F32)
    cos_t = jnp.concatenate([cos, cos, 1.0 + zeros], axis=1)
    sin_hi = jnp.concatenate([z16, sin, zeros], axis=1)
    sin_lo = jnp.concatenate([-sin, z16, zeros], axis=1)
    ident = jnp.ones((S, LANES), F32)
    zero = jnp.zeros((S, LANES), F32)
    return (jnp.stack([cos_t * Q_SCALE, cos_t, ident]),
            jnp.stack([sin_hi * Q_SCALE, sin_hi, zero]),
            jnp.stack([sin_lo * Q_SCALE, sin_lo, zero]))


def _forward(x, norm_w, final_norm_w, na_w_in, na_rpb, na_w_out, dl_w_in, dl_w_out, *,
             tm_norm=256, tm=512, tn=1024, tm_out=256):
    B, S, D = x.shape
    M = B * S
    assert norm_w.shape[0] == 2 and na_w_in.shape[0] == 1 and dl_w_in.shape[0] == 1
    na_width = na_w_out.shape[1]
    na_heads = na_width // HEAD_DIM
    dl_width = dl_w_out.shape[1]
    dl_heads = dl_width // HEAD_DIM
    n_groups = len(DL_GROUPS)

    w_in0 = na_w_in[0].astype(BF16)
    w_out0 = na_w_out[0].astype(BF16)
    (xn0,) = _rmsnorm(x, norm_w[0], (1,), tm_norm)
    xn0 = xn0.reshape(M, D)
    qkv0 = _matmul(xn0, w_in0, n_cols=3 * na_width, col_off=0, out_dtype=BF16, tm=tm, tn=tn,
                   n_q_cols=na_width, name="proj0_qkv")
    gate0 = _matmul(xn0, w_in0, n_cols=na_width, col_off=3 * na_width, out_dtype=F32, tm=tm, tn=tn,
                    name="proj0_gate")
    y0 = _na_attention(qkv0.reshape(B, S, 3 * na_width), gate0.reshape(B, S, na_width), na_rpb[0],
                       heads=na_heads)
    x1 = _matmul(y0.reshape(M, na_width), w_out0, n_cols=D, col_off=0, out_dtype=F32, tm=tm, tn=tn,
                 res=x.reshape(M, D), name="out0")

    w_in1 = dl_w_in[0].astype(BF16)
    w_out1 = dl_w_out[0].astype(BF16)
    dils = tuple(sorted({d for _, d in DL_GROUPS}))
    xn1 = dict(zip(dils, _rmsnorm(x1.reshape(B, S, D), norm_w[1], dils, tm_norm)))
    os_, lses = [], []
    for gi, (window, dil) in enumerate(DL_GROUPS):
        half = window // (2 * dil)
        qkv = _matmul(xn1[dil].reshape(M, D), w_in1, n_cols=3 * dl_width, col_off=gi * 3 * dl_width,
                      out_dtype=BF16, tm=min(tm, S // dil), tn=tn, rope=_rope_tables(S, dil),
                      n_q_cols=dl_width, name=f"proj1_g{gi}")
        o, lse = _dilated_attention(qkv.reshape(B, dil, S // dil, 3 * dl_width), B=B, dil=dil, half=half,
                                    heads=dl_heads)
        os_.append(o.reshape(M, dl_width))
        lses.append(lse.reshape(M, LANES))
    gate1 = _matmul(xn1[1].reshape(M, D), w_in1, n_cols=dl_width, col_off=3 * n_groups * dl_width,
                    out_dtype=F32, tm=tm, tn=tn, name="proj1_gate")
    out = _mix_out(os_, lses, gate1, w_out1, x1, final_norm_w, heads=dl_heads, tm=tm_out)
    return out.reshape(B, S, D)


def kernel(x, norm_w, final_norm_w, na_w_in, na_rpb, na_w_out, dl_w_in, dl_w_out):
    return _forward(x, norm_w, final_norm_w, na_w_in, na_rpb, na_w_out, dl_w_in, dl_w_out)
```

```python
import functools
import math
from typing import Any, NamedTuple

import jax
import jax.numpy as jnp
from jax import lax
from jax.experimental import pallas as pl
from jax.experimental.pallas import tpu as pltpu

HEAD_DIM = 128
GRID_W = 64
NA_KH = 8
NA_KW = 16
DL_GROUPS = ((128, 1), (512, 4), (2048, 16))
ROPE_THETA = 500000.0
ROPE_DIM = HEAD_DIM // 4
RMS_EPS = 1e-6
NEG_INF = -1e30
SCALE = HEAD_DIM ** -0.5
LOG2E = math.log2(math.e)
LN2 = math.log(2.0)
Q_SCALE = SCALE * LOG2E

LANES = 128
MXU_COLS = 256
VMEM_LIMIT_BYTES = 56 * 1024 * 1024

BF16 = jnp.bfloat16
F32 = jnp.float32


def _params(*semantics):
    return pltpu.CompilerParams(dimension_semantics=semantics, vmem_limit_bytes=VMEM_LIMIT_BYTES)


def _softmax_pv(s, v):
    m = jnp.max(s, axis=-1, keepdims=True)
    p = jnp.exp2(s - m).astype(BF16)
    v_aug = jnp.concatenate([v, jnp.ones_like(v)], axis=1)
    pv = jnp.dot(p, v_aug, preferred_element_type=F32)
    den = pv[:, HEAD_DIM:]
    return pv[:, :HEAD_DIM] / den, m, den


def _rmsnorm_kernel(x_ref, g_ref, *rest, dils):
    out_refs, y_ref = rest[:-1], rest[-1]
    x = x_ref[0]
    y = x * lax.rsqrt(jnp.mean(x * x, axis=-1, keepdims=True) + RMS_EPS)
    y = y * g_ref[...]
    tm, D = x.shape
    n_chunks = D // LANES
    if any(d > 1 for d in dils):
        for c in range(n_chunks):
            y_ref[c] = y[:, c * LANES:(c + 1) * LANES]
    for d, o_ref in zip(dils, out_refs):
        if d == 1:
            o_ref[0, 0] = y.astype(o_ref.dtype)
        else:
            for c in range(n_chunks):
                for r in range(d):
                    o_ref[0, r, :, c * LANES:(c + 1) * LANES] = (
                        y_ref[c, pl.ds(r, tm // d, stride=d), :].astype(o_ref.dtype))


def _rmsnorm(x, g, dils, tm):
    B, S, D = x.shape
    out_shape = [jax.ShapeDtypeStruct((B, d, S // d, D), BF16) for d in dils]
    out_specs = [pl.BlockSpec((1, d, tm // d, D), lambda b, i: (b, 0, i, 0)) for d in dils]
    return pl.pallas_call(
        functools.partial(_rmsnorm_kernel, dils=dils),
        grid=(B, S // tm),
        in_specs=[pl.BlockSpec((1, tm, D), lambda b, i: (b, i, 0)),
                  pl.BlockSpec((1, D), lambda b, i: (0, 0))],
        out_specs=out_specs,
        out_shape=out_shape,
        scratch_shapes=[pltpu.VMEM((D // LANES, tm, LANES), F32)],
        compiler_params=_params("parallel", "parallel"),
        name="rmsnorm",
    )(x, g.reshape(1, D))


CAST_ROWS = 256


class _Seg(NamedTuple):
    col_off: int
    n_cols: int
    tn: int
    out_dtype: Any
    mode: str = "plain"
    n_q_cols: int = 0


def _proj_kernel(a_ref, *refs, segs, has_res, has_rope):
    n = len(segs)
    w_refs, refs = refs[:n], refs[n:]
    if has_res:
        res_ref, refs = refs[0], refs[1:]
    if has_rope:
        (cos_ref, shi_ref, slo_ref), refs = refs[:3], refs[3:]
    o_refs, wb_refs = refs[:n], refs[n:]
    j = pl.program_id(0)

    @pl.when(pl.program_id(1) == 0)
    def _convert_weights():
        for w_ref, wb_ref in zip(w_refs, wb_refs):
            def body(r, carry, w_ref=w_ref, wb_ref=wb_ref):
                rows = pl.ds(pl.multiple_of(r * CAST_ROWS, CAST_ROWS), CAST_ROWS)
                wb_ref[rows, :] = w_ref[rows, :].astype(BF16)
                return carry
            lax.fori_loop(0, w_ref.shape[0] // CAST_ROWS, body, 0)

    half = ROPE_DIM // 2
    for seg, wb_ref, o_ref in zip(segs, wb_refs, o_refs):
        for c in range(seg.tn // MXU_COLS):
            cols = slice(c * MXU_COLS, (c + 1) * MXU_COLS)
            acc = jnp.dot(a_ref[...], wb_ref[:, cols], preferred_element_type=F32)
            if seg.mode == "res":
                acc = res_ref[:, cols] + acc
            elif seg.mode == "rope":
                heads = []
                for hh in range(MXU_COLS // LANES):
                    x = acc[:, hh * LANES:(hh + 1) * LANES]
                    heads.append(x * cos_ref[0] + pltpu.roll(x, half, 1) * shi_ref[0]
                                 + pltpu.roll(x, LANES - half, 1) * slo_ref[0])
                acc = jnp.concatenate(heads, axis=1)
            elif seg.n_q_cols:
                acc = acc * jnp.where(j * seg.tn + c * MXU_COLS < seg.n_q_cols, Q_SCALE, 1.0).astype(F32)
            o_ref[:, cols] = acc.astype(o_ref.dtype)


def _proj(a, w, segs, *, tm, res=None, rope=None, name):
    M, K = a.shape
    steps = segs[0].n_cols // segs[0].tn
    assert M % tm == 0 and K % CAST_ROWS == 0
    in_specs = [pl.BlockSpec((tm, K), lambda j, i: (i, 0))]
    args = [a]
    for seg in segs:
        assert seg.n_cols == steps * seg.tn and seg.col_off % seg.tn == 0 and seg.tn % MXU_COLS == 0
        in_specs.append(pl.BlockSpec((K, seg.tn), lambda j, i, off=seg.col_off // seg.tn: (0, j + off)))
        args.append(w)
    if res is not None:
        assert len(segs) == 1 and segs[0].mode == "res"
        in_specs.append(pl.BlockSpec((tm, segs[0].tn), lambda j, i: (i, j)))
        args.append(res)
    if rope is not None:
        (seg,) = [s for s in segs if s.mode == "rope"]
        P = rope[0].shape[1]
        assert P % tm == 0 and seg.n_q_cols % seg.tn == 0 and seg.n_cols == 2 * seg.n_q_cols
        nper, n_q_steps = P // tm, seg.n_q_cols // seg.tn
        for t in rope:
            in_specs.append(pl.BlockSpec((1, tm, LANES), lambda j, i: (j // n_q_steps, i % nper, 0)))
            args.append(t)
    return pl.pallas_call(
        functools.partial(_proj_kernel, segs=tuple(segs), has_res=res is not None, has_rope=rope is not None),
        grid=(steps, M // tm),
        in_specs=in_specs,
        out_specs=[pl.BlockSpec((tm, seg.tn), lambda j, i: (i, j)) for seg in segs],
        out_shape=[jax.ShapeDtypeStruct((M, seg.n_cols), seg.out_dtype) for seg in segs],
        scratch_shapes=[pltpu.VMEM((K, seg.tn), BF16) for seg in segs],
        compiler_params=_params("arbitrary", "arbitrary"),
        name=name,
    )(*args)


NA_QROWS = 8
NA_KROWS = 16


def _na_block_tables(rows):
    nblk = rows // NA_QROWS
    kb0s, vids, variants = [], [], []
    for i in range(nblk):
        kb0 = min(max(NA_QROWS * i - NA_KH // 2, 0), rows - NA_KROWS)
        tab = []
        for a in range(NA_QROWS):
            r = NA_QROWS * i + a
            rs = min(max(r - NA_KH // 2, 0), rows - NA_KH)
            for kl in range(NA_KROWS):
                kr = kb0 + kl
                tab.append(kr - r + NA_KH - 1 if rs <= kr < rs + NA_KH else None)
        tab = tuple(tab)
        if tab not in variants:
            variants.append(tab)
        kb0s.append(kb0)
        vids.append(variants.index(tab))
    return kb0s, vids, variants


def _na_kernel(rpb_ref, kb0_ref, vid_ref, q_ref, k_ref, v_ref, g_ref, o_ref, tile_ref, bias_ref,
               *, variants, nblk):
    h = pl.program_id(0)
    W = GRID_W
    n_dr, n_dc = 2 * NA_KH - 1, 2 * NA_KW - 1

    @pl.when(pl.program_id(1) == 0)
    def _build_bias():
        qc = lax.broadcasted_iota(jnp.int32, (W, LANES), 0)
        lane = lax.broadcasted_iota(jnp.int32, (W, LANES), 1)
        kc = lane & (W - 1)
        cs = jnp.clip(qc - NA_KW // 2, 0, W - NA_KW)
        col_ok = (kc >= cs) & (kc < cs + NA_KW)
        dc = jnp.clip(kc - qc + NA_KW - 1, 0, n_dc - 1)

        def build_tile(dr, carry):
            t = jnp.zeros((W, LANES), F32)
            base = (h * n_dr + dr) * n_dc
            for j in range(n_dc):
                t = jnp.where(dc == j, rpb_ref[base + j] * LOG2E, t)
            tile_ref[dr] = jnp.where(col_ok, t, NEG_INF)
            return carry

        lax.fori_loop(0, n_dr, build_tile, 0)
        neg = jnp.full((W, LANES), NEG_INF, F32)
        for vi, tab in enumerate(variants):
            for a in range(NA_QROWS):
                for p in range(NA_KROWS // 2):
                    d0 = tab[a * NA_KROWS + 2 * p]
                    d1 = tab[a * NA_KROWS + 2 * p + 1]
                    t0 = neg if d0 is None else tile_ref[d0]
                    t1 = neg if d1 is None else tile_ref[d1]
                    bias_ref[vi, a * W:(a + 1) * W, p * LANES:(p + 1) * LANES] = jnp.where(lane < W, t0, t1)

    nq, nk = NA_QROWS * W, NA_KROWS * W

    def block(i, carry):
        q0 = pl.multiple_of(i * nq, nq)
        k0 = pl.multiple_of(kb0_ref[i] * W, 2 * LANES)
        q = q_ref[0, pl.ds(q0, nq), :]
        k = k_ref[0, pl.ds(k0, nk), :]
        v = v_ref[0, pl.ds(k0, nk), :]
        s = lax.dot_general(q, k, (((1,), (1,)), ((), ())), preferred_element_type=F32)
        o, _, _ = _softmax_pv(s + bias_ref[vid_ref[i]], v)
        g = g_ref[0, pl.ds(q0, nq), :]
        o_ref[0, pl.ds(q0, nq), :] = (o * (g / (1.0 + jnp.exp(-g)))).astype(o_ref.dtype)
        return carry

    lax.fori_loop(0, nblk, block, 0, unroll=2)


def _na_attention(qkv, gate, rpb, *, heads):
    B, S, _ = qkv.shape
    rows = S // GRID_W
    assert rows % (2 * NA_QROWS) == 0 and rows >= NA_KROWS
    kb0s, vids, variants = _na_block_tables(rows)
    nblk = rows // NA_QROWS
    smem = pl.BlockSpec(memory_space=pltpu.SMEM)
    tok = lambda off: pl.BlockSpec((1, S, HEAD_DIM), lambda h, b: (b, 0, h + off))
    return pl.pallas_call(
        functools.partial(_na_kernel, variants=variants, nblk=nblk),
        grid=(heads, B),
        in_specs=[smem, smem, smem, tok(0), tok(heads), tok(2 * heads), tok(0)],
        out_specs=tok(0),
        out_shape=jax.ShapeDtypeStruct((B, S, heads * HEAD_DIM), BF16),
        scratch_shapes=[pltpu.VMEM((2 * NA_KH - 1, GRID_W, LANES), F32),
                        pltpu.VMEM((len(variants), NA_QROWS * GRID_W, NA_KROWS * GRID_W), F32)],
        compiler_params=_params("arbitrary", "arbitrary"),
        name="na_attention",
    )(rpb.reshape(-1), jnp.asarray(kb0s, jnp.int32), jnp.asarray(vids, jnp.int32), qkv, qkv, qkv, gate)


DL_QB = 256


def _dilated_kernel(q_ref, k_ref, v_ref, o_ref, lse_ref, mask_ref, *, dil, half, L):
    h = pl.program_id(1)
    qb = min(DL_QB, L - 2 * half)
    kb = qb + 2 * half
    nqb = L // qb

    @pl.when(h == 0)
    def _():
        lse_ref[...] = jnp.zeros_like(lse_ref)

    rel = (lax.broadcasted_iota(jnp.int32, (qb, kb), 1) - lax.broadcasted_iota(jnp.int32, (qb, kb), 0))
    for vi in range(3):
        mask_ref[vi] = jnp.where(jnp.abs(rel - vi * half) <= half, 0.0, NEG_INF)
    lane = lax.broadcasted_iota(jnp.int32, (qb, LANES), 1)

    def block(n, carry):
        r = n // nqb
        q0 = pl.multiple_of((n % nqb) * qb, qb)
        k0 = pl.multiple_of(jnp.clip(q0 - half, 0, L - kb), half)
        q = q_ref[0, r, pl.ds(q0, qb), :]
        k = k_ref[0, r, pl.ds(k0, kb), :]
        v = v_ref[0, r, pl.ds(k0, kb), :]
        s = lax.dot_general(q, k, (((1,), (1,)), ((), ())), preferred_element_type=F32)
        o, m, den = _softmax_pv(s + mask_ref[(q0 - k0) // half], v)
        lse = m * LN2 + jnp.log(den)
        if dil == 1:
            rows = pl.ds(q0, qb)
        else:
            rows = pl.ds(q0 * dil + r, qb, stride=dil)
        o_ref[0, rows, :] = o
        lse_ref[0, rows, :] = jnp.where(lane == h, lse, lse_ref[0, rows, :])
        return carry

    n_blocks = dil * nqb
    lax.fori_loop(0, n_blocks, block, 0, unroll=2 if n_blocks % 2 == 0 else 1)


def _dilated_attention(qk, v, *, B, dil, half, heads):
    L = qk.shape[2]
    S = L * dil
    qb = min(DL_QB, L - 2 * half)
    assert L % qb == 0 and heads <= LANES
    blk = lambda off: pl.BlockSpec((1, dil, L, HEAD_DIM), lambda b, h: (b, 0, 0, h + off))
    return pl.pallas_call(
        functools.partial(_dilated_kernel, dil=dil, half=half, L=L),
        grid=(B, heads),
        in_specs=[blk(0), blk(heads), blk(0)],
        out_specs=[pl.BlockSpec((1, S, HEAD_DIM), lambda b, h: (b, 0, h)),
                   pl.BlockSpec((1, S, LANES), lambda b, h: (b, 0, 0))],
        out_shape=[jax.ShapeDtypeStruct((B, S, heads * HEAD_DIM), F32),
                   jax.ShapeDtypeStruct((B, S, LANES), F32)],
        scratch_shapes=[pltpu.VMEM((3, qb, qb + 2 * half), F32)],
        compiler_params=_params("arbitrary", "arbitrary"),
        name=f"dilated_attention_d{dil}",
    )(qk, qk, v)


def _mix_out_kernel(*refs, n_groups, heads):
    o_refs = refs[:n_groups]
    l_refs = refs[n_groups:2 * n_groups]
    g_ref, w_ref, x_ref, nw_ref, out_ref, y_ref = refs[2 * n_groups:]
    lses = [r[...] for r in l_refs]
    mx = functools.reduce(jnp.maximum, lses)
    es = [jnp.exp(l - mx) for l in lses]
    inv = 1.0 / functools.reduce(lambda a, b: a + b, es)
    wts = [e * inv for e in es]
    for h in range(heads):
        sl = slice(h * HEAD_DIM, (h + 1) * HEAD_DIM)
        o = None
        for wt, o_ref in zip(wts, o_refs):
            t = wt[:, h:h + 1] * o_ref[:, sl]
            o = t if o is None else o + t
        g = g_ref[:, sl]
        y_ref[:, sl] = (o * (g / (1.0 + jnp.exp(-g)))).astype(y_ref.dtype)
    x = x_ref[...] + jnp.dot(y_ref[...], w_ref[...], preferred_element_type=F32)
    y = x * lax.rsqrt(jnp.mean(x * x, axis=-1, keepdims=True) + RMS_EPS)
    out_ref[...] = y * nw_ref[...]


def _mix_out(os_, lses, gate, w, x, norm_w, *, heads, tm):
    M, D = x.shape
    width = heads * HEAD_DIM
    n = len(os_)
    row = lambda c: pl.BlockSpec((tm, c), lambda i: (i, 0))
    return pl.pallas_call(
        functools.partial(_mix_out_kernel, n_groups=n, heads=heads),
        grid=(M // tm,),
        in_specs=[row(width)] * n + [row(LANES)] * n + [
            row(width),
            pl.BlockSpec((width, D), lambda i: (0, 0), pipeline_mode=pl.Buffered(1)),
            row(D),
            pl.BlockSpec((1, D), lambda i: (0, 0))],
        out_specs=row(D),
        out_shape=jax.ShapeDtypeStruct((M, D), F32),
        scratch_shapes=[pltpu.VMEM((tm, width), BF16)],
        compiler_params=_params("parallel"),
        name="mix_out_norm",
    )(*os_, *lses, gate, w, x, norm_w.reshape(1, D))


def _rope_tables(S, dil):
    half = ROPE_DIM // 2
    pos = jnp.arange(S).reshape(S // dil, dil).T.reshape(S)
    inv_freq = jnp.power(ROPE_THETA, -jnp.arange(half, dtype=F32) * (2.0 / ROPE_DIM))
    ang = pos.astype(F32)[:, None] * inv_freq[None, :]
    cos, sin = jnp.cos(ang), jnp.sin(ang)
    zeros = jnp.zeros((S, LANES - ROPE_DIM), F32)
    z16 = jnp.zeros((S, half), F32)
    cos_t = jnp.concatenate([cos, cos, 1.0 + zeros], axis=1)
    sin_hi = jnp.concatenate([z16, sin, zeros], axis=1)
    sin_lo = jnp.concatenate([-sin, z16, zeros], axis=1)
    return tuple(jnp.stack([t * Q_SCALE, t]) for t in (cos_t, sin_hi, sin_lo))


def _forward(x, norm_w, final_norm_w, na_w_in, na_rpb, na_w_out, dl_w_in, dl_w_out, *,
             tm_norm=256, tm=512, tm_plain=1024, tn_qkv0=768, tn=512, tm_out=256):
    B, S, D = x.shape
    M = B * S
    assert norm_w.shape[0] == 2 and na_w_in.shape[0] == 1 and dl_w_in.shape[0] == 1
    na_width = na_w_out.shape[1]
    na_heads = na_width // HEAD_DIM
    dl_width = dl_w_out.shape[1]
    dl_heads = dl_width // HEAD_DIM
    n_groups = len(DL_GROUPS)

    (xn0,) = _rmsnorm(x, norm_w[0], (1,), tm_norm)
    xn0 = xn0.reshape(M, D)
    (qkv0,) = _proj(xn0, na_w_in[0], [_Seg(0, 3 * na_width, tn_qkv0, BF16, n_q_cols=na_width)], tm=tm_plain,
                    name="proj0_qkv")
    (gate0,) = _proj(xn0, na_w_in[0], [_Seg(3 * na_width, na_width, tn, F32)], tm=tm_plain, name="proj0_gate")
    y0 = _na_attention(qkv0.reshape(B, S, 3 * na_width), gate0.reshape(B, S, na_width), na_rpb[0],
                       heads=na_heads)
    (x1,) = _proj(y0.reshape(M, na_width), na_w_out[0], [_Seg(0, D, tn, F32, mode="res")], tm=tm_plain,
                  res=x.reshape(M, D), name="out0")

    w_in1 = dl_w_in[0]
    dils = tuple(sorted({d for _, d in DL_GROUPS}))
    xn1 = dict(zip(dils, _rmsnorm(x1.reshape(B, S, D), norm_w[1], dils, tm_norm)))
    os_, lses = [], []
    for gi, (window, dil) in enumerate(DL_GROUPS):
        half = window // (2 * dil)
        off = gi * 3 * dl_width
        qk, v = _proj(xn1[dil].reshape(M, D), w_in1,
                      [_Seg(off, 2 * dl_width, tn, BF16, mode="rope", n_q_cols=dl_width),
                       _Seg(off + 2 * dl_width, dl_width, tn // 2, BF16)],
                      tm=min(tm, S // dil), rope=_rope_tables(S, dil), name=f"proj1_g{gi}")
        o, lse = _dilated_attention(qk.reshape(B, dil, S // dil, 2 * dl_width),
                                    v.reshape(B, dil, S // dil, dl_width), B=B, dil=dil, half=half,
                                    heads=dl_heads)
        os_.append(o.reshape(M, dl_width))
        lses.append(lse.reshape(M, LANES))
    (gate1,) = _proj(xn1[1].reshape(M, D), w_in1, [_Seg(3 * n_groups * dl_width, dl_width, tn, F32)], tm=tm_plain,
                     name="proj1_gate")
    out = _mix_out(os_, lses, gate1, dl_w_out[0].astype(BF16), x1, final_norm_w, heads=dl_heads, tm=tm_out)
    return out.reshape(B, S, D)


def kernel(x, norm_w, final_norm_w, na_w_in, na_rpb, na_w_out, dl_w_in, dl_w_out):
    return _forward(x, norm_w, final_norm_w, na_w_in, na_rpb, na_w_out, dl_w_in, dl_w_out)
```

```python
import functools
import math
from typing import Any, NamedTuple

import jax
import jax.numpy as jnp
from jax import lax
from jax.experimental import pallas as pl
from jax.experimental.pallas import tpu as pltpu

HEAD_DIM = 128
GRID_W = 64
NA_KH = 8
NA_KW = 16
DL_GROUPS = ((128, 1), (512, 4), (2048, 16))
ROPE_THETA = 500000.0
ROPE_DIM = HEAD_DIM // 4
RMS_EPS = 1e-6
NEG_INF = -1e30
SCALE = HEAD_DIM ** -0.5
LOG2E = math.log2(math.e)
LN2 = math.log(2.0)
Q_SCALE = SCALE * LOG2E

LANES = 128
MXU_COLS = 256
VMEM_LIMIT_BYTES = 56 * 1024 * 1024

BF16 = jnp.bfloat16
F32 = jnp.float32


def _params(*semantics):
    return pltpu.CompilerParams(dimension_semantics=semantics, vmem_limit_bytes=VMEM_LIMIT_BYTES)


def _attend(blocks):
    logits = [lax.dot_general(q, k, (((1,), (1,)), ((), ())), preferred_element_type=F32) + bias
              for q, k, _, bias in blocks]
    out = []
    for s, (_, _, v, _) in zip(logits, blocks):
        m = jnp.max(s, axis=-1, keepdims=True)
        p = jnp.exp2(s - m).astype(BF16)
        v_aug = jnp.concatenate([v, jnp.ones_like(v)], axis=1)
        pv = jnp.dot(p, v_aug, preferred_element_type=F32)
        den = pv[:, HEAD_DIM:]
        out.append((pv[:, :HEAD_DIM] / den, m, den))
    return out


def _rmsnorm_kernel(x_ref, g_ref, *rest, dils):
    out_refs, y_ref = rest[:-1], rest[-1]
    x = x_ref[0]
    y = x * lax.rsqrt(jnp.mean(x * x, axis=-1, keepdims=True) + RMS_EPS)
    y = y * g_ref[...]
    tm, D = x.shape
    n_chunks = D // LANES
    if any(d > 1 for d in dils):
        for c in range(n_chunks):
            y_ref[c] = y[:, c * LANES:(c + 1) * LANES]
    for d, o_ref in zip(dils, out_refs):
        if d == 1:
            o_ref[0, 0] = y.astype(o_ref.dtype)
        else:
            for c in range(n_chunks):
                for r in range(d):
                    o_ref[0, r, :, c * LANES:(c + 1) * LANES] = (
                        y_ref[c, pl.ds(r, tm // d, stride=d), :].astype(o_ref.dtype))


def _rmsnorm(x, g, dils, tm):
    B, S, D = x.shape
    out_shape = [jax.ShapeDtypeStruct((B, d, S // d, D), BF16) for d in dils]
    out_specs = [pl.BlockSpec((1, d, tm // d, D), lambda b, i: (b, 0, i, 0)) for d in dils]
    return pl.pallas_call(
        functools.partial(_rmsnorm_kernel, dils=dils),
        grid=(B, S // tm),
        in_specs=[pl.BlockSpec((1, tm, D), lambda b, i: (b, i, 0)),
                  pl.BlockSpec((1, D), lambda b, i: (0, 0))],
        out_specs=out_specs,
        out_shape=out_shape,
        scratch_shapes=[pltpu.VMEM((D // LANES, tm, LANES), F32)],
        compiler_params=_params("parallel", "parallel"),
        name="rmsnorm",
    )(x, g.reshape(1, D))


CAST_ROWS = 256


class _Seg(NamedTuple):
    col_off: int
    n_cols: int
    tn: int
    out_dtype: Any
    mode: str = "plain"
    n_q_cols: int = 0


def _proj_kernel(a_ref, *refs, segs, has_res, has_rope):
    n = len(segs)
    w_refs, refs = refs[:n], refs[n:]
    if has_res:
        res_ref, refs = refs[0], refs[1:]
    if has_rope:
        (cos_ref, shi_ref, slo_ref), refs = refs[:3], refs[3:]
    o_refs, wb_refs = refs[:n], refs[n:]
    j = pl.program_id(0)

    @pl.when(pl.program_id(1) == 0)
    def _convert_weights():
        for w_ref, wb_ref in zip(w_refs, wb_refs):
            def body(r, carry, w_ref=w_ref, wb_ref=wb_ref):
                rows = pl.ds(pl.multiple_of(r * CAST_ROWS, CAST_ROWS), CAST_ROWS)
                wb_ref[rows, :] = w_ref[rows, :].astype(BF16)
                return carry
            lax.fori_loop(0, w_ref.shape[0] // CAST_ROWS, body, 0)

    half = ROPE_DIM // 2
    for seg, wb_ref, o_ref in zip(segs, wb_refs, o_refs):
        for c in range(seg.tn // MXU_COLS):
            cols = slice(c * MXU_COLS, (c + 1) * MXU_COLS)
            acc = jnp.dot(a_ref[...], wb_ref[:, cols], preferred_element_type=F32)
            if seg.mode == "res":
                acc = res_ref[:, cols] + acc
            elif seg.mode == "rope":
                heads = []
                for hh in range(MXU_COLS // LANES):
                    x = acc[:, hh * LANES:(hh + 1) * LANES]
                    heads.append(x * cos_ref[0] + pltpu.roll(x, half, 1) * shi_ref[0]
                                 + pltpu.roll(x, LANES - half, 1) * slo_ref[0])
                acc = jnp.concatenate(heads, axis=1)
            elif seg.n_q_cols:
                acc = acc * jnp.where(j * seg.tn + c * MXU_COLS < seg.n_q_cols, Q_SCALE, 1.0).astype(F32)
            o_ref[:, cols] = acc.astype(o_ref.dtype)


def _proj(a, w, segs, *, tm, res=None, rope=None, name):
    M, K = a.shape
    steps = segs[0].n_cols // segs[0].tn
    assert M % tm == 0 and K % CAST_ROWS == 0
    in_specs = [pl.BlockSpec((tm, K), lambda j, i: (i, 0))]
    args = [a]
    for seg in segs:
        assert seg.n_cols == steps * seg.tn and seg.col_off % seg.tn == 0 and seg.tn % MXU_COLS == 0
        in_specs.append(pl.BlockSpec((K, seg.tn), lambda j, i, off=seg.col_off // seg.tn: (0, j + off)))
        args.append(w)
    if res is not None:
        assert len(segs) == 1 and segs[0].mode == "res"
        in_specs.append(pl.BlockSpec((tm, segs[0].tn), lambda j, i: (i, j)))
        args.append(res)
    if rope is not None:
        (seg,) = [s for s in segs if s.mode == "rope"]
        P = rope[0].shape[1]
        assert P % tm == 0 and seg.n_q_cols % seg.tn == 0 and seg.n_cols == 2 * seg.n_q_cols
        nper, n_q_steps = P // tm, seg.n_q_cols // seg.tn
        for t in rope:
            in_specs.append(pl.BlockSpec((1, tm, LANES), lambda j, i: (j // n_q_steps, i % nper, 0)))
            args.append(t)
    return pl.pallas_call(
        functools.partial(_proj_kernel, segs=tuple(segs), has_res=res is not None, has_rope=rope is not None),
        grid=(steps, M // tm),
        in_specs=in_specs,
        out_specs=[pl.BlockSpec((tm, seg.tn), lambda j, i: (i, j)) for seg in segs],
        out_shape=[jax.ShapeDtypeStruct((M, seg.n_cols), seg.out_dtype) for seg in segs],
        scratch_shapes=[pltpu.VMEM((K, seg.tn), BF16) for seg in segs],
        compiler_params=_params("arbitrary", "arbitrary"),
        name=name,
    )(*args)


NA_QROWS = 8
NA_KROWS = 16
NA_UNROLL = 4


def _na_block_tables(rows):
    nblk = rows // NA_QROWS
    kb0s, vids, variants = [], [], []
    for i in range(nblk):
        kb0 = min(max(NA_QROWS * i - NA_KH // 2, 0), rows - NA_KROWS)
        tab = []
        for a in range(NA_QROWS):
            r = NA_QROWS * i + a
            rs = min(max(r - NA_KH // 2, 0), rows - NA_KH)
            for kl in range(NA_KROWS):
                kr = kb0 + kl
                tab.append(kr - r + NA_KH - 1 if rs <= kr < rs + NA_KH else None)
        tab = tuple(tab)
        if tab not in variants:
            variants.append(tab)
        kb0s.append(kb0)
        vids.append(variants.index(tab))
    return kb0s, vids, variants


def _na_kernel(rpb_ref, kb0_ref, vid_ref, q_ref, k_ref, v_ref, g_ref, o_ref, tile_ref, bias_ref,
               *, variants, nblk):
    h = pl.program_id(0)
    W = GRID_W
    n_dr, n_dc = 2 * NA_KH - 1, 2 * NA_KW - 1

    @pl.when(pl.program_id(1) == 0)
    def _build_bias():
        qc = lax.broadcasted_iota(jnp.int32, (W, LANES), 0)
        lane = lax.broadcasted_iota(jnp.int32, (W, LANES), 1)
        kc = lane & (W - 1)
        cs = jnp.clip(qc - NA_KW // 2, 0, W - NA_KW)
        col_ok = (kc >= cs) & (kc < cs + NA_KW)
        dc = jnp.clip(kc - qc + NA_KW - 1, 0, n_dc - 1)

        def build_tile(dr, carry):
            t = jnp.zeros((W, LANES), F32)
            base = (h * n_dr + dr) * n_dc
            for j in range(n_dc):
                t = jnp.where(dc == j, rpb_ref[base + j] * LOG2E, t)
            tile_ref[dr] = jnp.where(col_ok, t, NEG_INF)
            return carry

        lax.fori_loop(0, n_dr, build_tile, 0)
        neg = jnp.full((W, LANES), NEG_INF, F32)
        for vi, tab in enumerate(variants):
            for a in range(NA_QROWS):
                for p in range(NA_KROWS // 2):
                    d0 = tab[a * NA_KROWS + 2 * p]
                    d1 = tab[a * NA_KROWS + 2 * p + 1]
                    t0 = neg if d0 is None else tile_ref[d0]
                    t1 = neg if d1 is None else tile_ref[d1]
                    bias_ref[vi, a * W:(a + 1) * W, p * LANES:(p + 1) * LANES] = jnp.where(lane < W, t0, t1)

    nq, nk = NA_QROWS * W, NA_KROWS * W

    def load(i):
        q0 = pl.multiple_of(i * nq, nq)
        k0 = pl.multiple_of(kb0_ref[i] * W, 2 * LANES)
        return (q_ref[0, pl.ds(q0, nq), :], k_ref[0, pl.ds(k0, nk), :], v_ref[0, pl.ds(k0, nk), :],
                bias_ref[vid_ref[i]])

    def group(n, carry):
        blocks = [n * NA_UNROLL + u for u in range(NA_UNROLL)]
        for i, (o, _, _) in zip(blocks, _attend([load(i) for i in blocks])):
            q0 = pl.multiple_of(i * nq, nq)
            g = g_ref[0, pl.ds(q0, nq), :]
            o_ref[0, pl.ds(q0, nq), :] = (o * (g / (1.0 + jnp.exp(-g)))).astype(o_ref.dtype)
        return carry

    lax.fori_loop(0, nblk // NA_UNROLL, group, 0)


def _na_attention(qkv, gate, rpb, *, heads):
    B, S, _ = qkv.shape
    rows = S // GRID_W
    assert rows % (2 * NA_QROWS) == 0 and rows >= NA_KROWS
    kb0s, vids, variants = _na_block_tables(rows)
    nblk = rows // NA_QROWS
    smem = pl.BlockSpec(memory_space=pltpu.SMEM)
    tok = lambda off: pl.BlockSpec((1, S, HEAD_DIM), lambda h, b: (b, 0, h + off))
    return pl.pallas_call(
        functools.partial(_na_kernel, variants=variants, nblk=nblk),
        grid=(heads, B),
        in_specs=[smem, smem, smem, tok(0), tok(heads), tok(2 * heads), tok(0)],
        out_specs=tok(0),
        out_shape=jax.ShapeDtypeStruct((B, S, heads * HEAD_DIM), BF16),
        scratch_shapes=[pltpu.VMEM((2 * NA_KH - 1, GRID_W, LANES), F32),
                        pltpu.VMEM((len(variants), NA_QROWS * GRID_W, NA_KROWS * GRID_W), F32)],
        compiler_params=_params("arbitrary", "arbitrary"),
        name="na_attention",
    )(rpb.reshape(-1), jnp.asarray(kb0s, jnp.int32), jnp.asarray(vids, jnp.int32), qkv, qkv, qkv, gate)


DL_QB = 256
DL_UNROLL = 8


def _dilated_kernel(q_ref, k_ref, v_ref, o_ref, lse_ref, mask_ref, *, dil, half, L):
    h = pl.program_id(1)
    qb = min(DL_QB, L - 2 * half)
    kb = qb + 2 * half
    nqb = L // qb

    @pl.when(h == 0)
    def _():
        lse_ref[...] = jnp.zeros_like(lse_ref)

    rel = (lax.broadcasted_iota(jnp.int32, (qb, kb), 1) - lax.broadcasted_iota(jnp.int32, (qb, kb), 0))
    for vi in range(3):
        mask_ref[vi] = jnp.where(jnp.abs(rel - vi * half) <= half, 0.0, NEG_INF)
    lane = lax.broadcasted_iota(jnp.int32, (qb, LANES), 1)

    def load(n):
        r = n // nqb
        q0 = pl.multiple_of((n % nqb) * qb, qb)
        k0 = pl.multiple_of(jnp.clip(q0 - half, 0, L - kb), half)
        return (q_ref[0, r, pl.ds(q0, qb), :], k_ref[0, r, pl.ds(k0, kb), :], v_ref[0, r, pl.ds(k0, kb), :],
                mask_ref[(q0 - k0) // half])

    unroll = math.gcd(dil * nqb, DL_UNROLL)

    def group(t, carry):
        blocks = [t * unroll + u for u in range(unroll)]
        for n, (o, m, den) in zip(blocks, _attend([load(n) for n in blocks])):
            r = n // nqb
            q0 = pl.multiple_of((n % nqb) * qb, qb)
            lse = m * LN2 + jnp.log(den)
            if dil == 1:
                rows = pl.ds(q0, qb)
            else:
                rows = pl.ds(q0 * dil + r, qb, stride=dil)
            o_ref[0, rows, :] = o
            lse_ref[0, rows, :] = jnp.where(lane == h, lse, lse_ref[0, rows, :])
        return carry

    lax.fori_loop(0, dil * nqb // unroll, group, 0)


def _dilated_attention(qk, v, *, B, dil, half, heads):
    L = qk.shape[2]
    S = L * dil
    qb = min(DL_QB, L - 2 * half)
    assert L % qb == 0 and heads <= LANES
    blk = lambda off: pl.BlockSpec((1, dil, L, HEAD_DIM), lambda b, h: (b, 0, 0, h + off))
    return pl.pallas_call(
        functools.partial(_dilated_kernel, dil=dil, half=half, L=L),
        grid=(B, heads),
        in_specs=[blk(0), blk(heads), blk(0)],
        out_specs=[pl.BlockSpec((1, S, HEAD_DIM), lambda b, h: (b, 0, h)),
                   pl.BlockSpec((1, S, LANES), lambda b, h: (b, 0, 0))],
        out_shape=[jax.ShapeDtypeStruct((B, S, heads * HEAD_DIM), F32),
                   jax.ShapeDtypeStruct((B, S, LANES), F32)],
        scratch_shapes=[pltpu.VMEM((3, qb, qb + 2 * half), F32)],
        compiler_params=_params("arbitrary", "arbitrary"),
        name=f"dilated_attention_d{dil}",
    )(qk, qk, v)


def _mix_out_kernel(*refs, n_groups, heads):
    o_refs = refs[:n_groups]
    l_refs = refs[n_groups:2 * n_groups]
    g_ref, w_ref, x_ref, nw_ref, out_ref, y_ref = refs[2 * n_groups:]
    lses = [r[...] for r in l_refs]
    mx = functools.reduce(jnp.maximum, lses)
    es = [jnp.exp(l - mx) for l in lses]
    inv = 1.0 / functools.reduce(lambda a, b: a + b, es)
    wts = [e * inv for e in es]
    for h in range(heads):
        sl = slice(h * HEAD_DIM, (h + 1) * HEAD_DIM)
        o = None
        for wt, o_ref in zip(wts, o_refs):
            t = wt[:, h:h + 1] * o_ref[:, sl]
            o = t if o is None else o + t
        g = g_ref[:, sl]
        y_ref[:, sl] = (o * (g / (1.0 + jnp.exp(-g)))).astype(y_ref.dtype)
    x = x_ref[...] + jnp.dot(y_ref[...], w_ref[...], preferred_element_type=F32)
    y = x * lax.rsqrt(jnp.mean(x * x, axis=-1, keepdims=True) + RMS_EPS)
    out_ref[...] = y * nw_ref[...]


def _mix_out(os_, lses, gate, w, x, norm_w, *, heads, tm):
    M, D = x.shape
    width = heads * HEAD_DIM
    n = len(os_)
    row = lambda c: pl.BlockSpec((tm, c), lambda i: (i, 0))
    return pl.pallas_call(
        functools.partial(_mix_out_kernel, n_groups=n, heads=heads),
        grid=(M // tm,),
        in_specs=[row(width)] * n + [row(LANES)] * n + [
            row(width),
            pl.BlockSpec((width, D), lambda i: (0, 0), pipeline_mode=pl.Buffered(1)),
            row(D),
            pl.BlockSpec((1, D), lambda i: (0, 0))],
        out_specs=row(D),
        out_shape=jax.ShapeDtypeStruct((M, D), F32),
        scratch_shapes=[pltpu.VMEM((tm, width), BF16)],
        compiler_params=_params("parallel"),
        name="mix_out_norm",
    )(*os_, *lses, gate, w, x, norm_w.reshape(1, D))


def _rope_tables(S, dil):
    half = ROPE_DIM // 2
    pos = jnp.arange(S).reshape(S // dil, dil).T.reshape(S)
    inv_freq = jnp.power(ROPE_THETA, -jnp.arange(half, dtype=F32) * (2.0 / ROPE_DIM))
    ang = pos.astype(F32)[:, None] * inv_freq[None, :]
    cos, sin = jnp.cos(ang), jnp.sin(ang)
    zeros = jnp.zeros((S, LANES - ROPE_DIM), F32)
    z16 = jnp.zeros((S, half), F32)
    cos_t = jnp.concatenate([cos, cos, 1.0 + zeros], axis=1)
    sin_hi = jnp.concatenate([z16, sin, zeros], axis=1)
    sin_lo = jnp.concatenate([-sin, z16, zeros], axis=1)
    return tuple(jnp.stack([t * Q_SCALE, t]) for t in (cos_t, sin_hi, sin_lo))


def _forward(x, norm_w, final_norm_w, na_w_in, na_rpb, na_w_out, dl_w_in, dl_w_out, *,
             tm_norm=256, tm=512, tm_plain=1024, tn_qkv0=768, tn=512, tm_out=256):
    B, S, D = x.shape
    M = B * S
    assert norm_w.shape[0] == 2 and na_w_in.shape[0] == 1 and dl_w_in.shape[0] == 1
    na_width = na_w_out.shape[1]
    na_heads = na_width // HEAD_DIM
    dl_width = dl_w_out.shape[1]
    dl_heads = dl_width // HEAD_DIM
    n_groups = len(DL_GROUPS)

    (xn0,) = _rmsnorm(x, norm_w[0], (1,), tm_norm)
    xn0 = xn0.reshape(M, D)
    (qkv0,) = _proj(xn0, na_w_in[0], [_Seg(0, 3 * na_width, tn_qkv0, BF16, n_q_cols=na_width)], tm=tm_plain,
                    name="proj0_qkv")
    (gate0,) = _proj(xn0, na_w_in[0], [_Seg(3 * na_width, na_width, tn, F32)], tm=tm_plain, name="proj0_gate")
    y0 = _na_attention(qkv0.reshape(B, S, 3 * na_width), gate0.reshape(B, S, na_width), na_rpb[0],
                       heads=na_heads)
    (x1,) = _proj(y0.reshape(M, na_width), na_w_out[0], [_Seg(0, D, tn, F32, mode="res")], tm=tm_plain,
                  res=x.reshape(M, D), name="out0")

    w_in1 = dl_w_in[0]
    dils = tuple(sorted({d for _, d in DL_GROUPS}))
    xn1 = dict(zip(dils, _rmsnorm(x1.reshape(B, S, D), norm_w[1], dils, tm_norm)))
    os_, lses = [], []
    for gi, (window, dil) in enumerate(DL_GROUPS):
        half = window // (2 * dil)
        off = gi * 3 * dl_width
        qk, v = _proj(xn1[dil].reshape(M, D), w_in1,
                      [_Seg(off, 2 * dl_width, tn, BF16, mode="rope", n_q_cols=dl_width),
                       _Seg(off + 2 * dl_width, dl_width, tn // 2, BF16)],
                      tm=min(tm, S // dil), rope=_rope_tables(S, dil), name=f"proj1_g{gi}")
        o, lse = _dilated_attention(qk.reshape(B, dil, S // dil, 2 * dl_width),
                                    v.reshape(B, dil, S // dil, dl_width), B=B, dil=dil, half=half,
                                    heads=dl_heads)
        os_.append(o.reshape(M, dl_width))
        lses.append(lse.reshape(M, LANES))
    (gate1,) = _proj(xn1[1].reshape(M, D), w_in1, [_Seg(3 * n_groups * dl_width, dl_width, tn, F32)], tm=tm_plain,
                     name="proj1_gate")
    out = _mix_out(os_, lses, gate1, dl_w_out[0].astype(BF16), x1, final_norm_w, heads=dl_heads, tm=tm_out)
    return out.reshape(B, S, D)


def kernel(x, norm_w, final_norm_w, na_w_in, na_rpb, na_w_out, dl_w_in, dl_w_out):
    return _forward(x, norm_w, final_norm_w, na_w_in, na_rpb, na_w_out, dl_w_in, dl_w_out)
```

```python
import functools
import math
from typing import Any, NamedTuple

import jax
import jax.numpy as jnp
from jax import lax
from jax.experimental import pallas as pl
from jax.experimental.pallas import tpu as pltpu

HEAD_DIM = 128
GRID_W = 64
NA_KH = 8
NA_KW = 16
DL_GROUPS = ((128, 1), (512, 4), (2048, 16))
ROPE_THETA = 500000.0
ROPE_DIM = HEAD_DIM // 4
RMS_EPS = 1e-6
NEG_INF = -1e30
SCALE = HEAD_DIM ** -0.5
LOG2E = math.log2(math.e)
LN2 = math.log(2.0)
Q_SCALE = SCALE * LOG2E

LANES = 128
MXU_COLS = 256
VMEM_LIMIT_BYTES = 56 * 1024 * 1024

BF16 = jnp.bfloat16
F32 = jnp.float32


def _params(*semantics):
    return pltpu.CompilerParams(dimension_semantics=semantics, vmem_limit_bytes=VMEM_LIMIT_BYTES)


def _attend(blocks):
    logits = [lax.dot_general(q, k, (((1,), (1,)), ((), ())), preferred_element_type=F32) + bias
              for q, k, _, bias in blocks]
    out = []
    for s, (_, _, v, _) in zip(logits, blocks):
        m = jnp.max(s, axis=-1, keepdims=True)
        p = jnp.exp2(s - m).astype(BF16)
        v_aug = jnp.concatenate([v, jnp.ones_like(v)], axis=1)
        pv = jnp.dot(p, v_aug, preferred_element_type=F32)
        den = pv[:, HEAD_DIM:]
        out.append((pv[:, :HEAD_DIM] / den, m, den))
    return out


def _rmsnorm_kernel(x_ref, g_ref, *rest, dils):
    out_refs, y_ref = rest[:-1], rest[-1]
    x = x_ref[0]
    y = x * lax.rsqrt(jnp.mean(x * x, axis=-1, keepdims=True) + RMS_EPS)
    y = y * g_ref[...]
    tm, D = x.shape
    n_chunks = D // LANES
    if any(d > 1 for d in dils):
        for c in range(n_chunks):
            y_ref[c] = y[:, c * LANES:(c + 1) * LANES]
    for d, o_ref in zip(dils, out_refs):
        if d == 1:
            o_ref[0, 0] = y.astype(o_ref.dtype)
        else:
            for c in range(n_chunks):
                for r in range(d):
                    o_ref[0, r, :, c * LANES:(c + 1) * LANES] = (
                        y_ref[c, pl.ds(r, tm // d, stride=d), :].astype(o_ref.dtype))


def _rmsnorm(x, g, dils, tm):
    B, S, D = x.shape
    out_shape = [jax.ShapeDtypeStruct((B, d, S // d, D), BF16) for d in dils]
    out_specs = [pl.BlockSpec((1, d, tm // d, D), lambda b, i: (b, 0, i, 0)) for d in dils]
    return pl.pallas_call(
        functools.partial(_rmsnorm_kernel, dils=dils),
        grid=(B, S // tm),
        in_specs=[pl.BlockSpec((1, tm, D), lambda b, i: (b, i, 0)),
                  pl.BlockSpec((1, D), lambda b, i: (0, 0))],
        out_specs=out_specs,
        out_shape=out_shape,
        scratch_shapes=[pltpu.VMEM((D // LANES, tm, LANES), F32)],
        compiler_params=_params("parallel", "parallel"),
        name="rmsnorm",
    )(x, g.reshape(1, D))


CAST_ROWS = 256


class _Seg(NamedTuple):
    col_off: int
    n_cols: int
    tn: int
    out_dtype: Any
    mode: str = "plain"
    n_q_cols: int = 0


def _proj_kernel(a_ref, *refs, segs, has_res, has_rope):
    n = len(segs)
    w_refs, refs = refs[:n], refs[n:]
    if has_res:
        res_ref, refs = refs[0], refs[1:]
    if has_rope:
        (cos_ref, shi_ref, slo_ref), refs = refs[:3], refs[3:]
    o_refs, wb_refs = refs[:n], refs[n:]
    j = pl.program_id(0)

    @pl.when(pl.program_id(1) == 0)
    def _convert_weights():
        for w_ref, wb_ref in zip(w_refs, wb_refs):
            def body(r, carry, w_ref=w_ref, wb_ref=wb_ref):
                rows = pl.ds(pl.multiple_of(r * CAST_ROWS, CAST_ROWS), CAST_ROWS)
                wb_ref[rows, :] = w_ref[rows, :].astype(BF16)
                return carry
            lax.fori_loop(0, w_ref.shape[0] // CAST_ROWS, body, 0)

    half = ROPE_DIM // 2
    for seg, wb_ref, o_ref in zip(segs, wb_refs, o_refs):
        for c in range(seg.tn // MXU_COLS):
            cols = slice(c * MXU_COLS, (c + 1) * MXU_COLS)
            acc = jnp.dot(a_ref[...], wb_ref[:, cols], preferred_element_type=F32)
            if seg.mode == "res":
                acc = res_ref[:, cols] + acc
            elif seg.mode == "rope":
                fac = jnp.where(j * seg.tn < seg.n_q_cols, Q_SCALE, 1.0).astype(F32)
                heads = []
                for hh in range(MXU_COLS // LANES):
                    x = acc[:, hh * LANES:(hh + 1) * LANES] * fac
                    heads.append(x * cos_ref[...] + pltpu.roll(x, half, 1) * shi_ref[...]
                                 + pltpu.roll(x, LANES - half, 1) * slo_ref[...])
                acc = jnp.concatenate(heads, axis=1)
            elif seg.n_q_cols:
                acc = acc * jnp.where(j * seg.tn + c * MXU_COLS < seg.n_q_cols, Q_SCALE, 1.0).astype(F32)
            o_ref[:, cols] = acc.astype(o_ref.dtype)


def _proj(a, w, segs, *, tm, res=None, rope=None, name):
    M, K = a.shape
    steps = segs[0].n_cols // segs[0].tn
    assert M % tm == 0 and K % CAST_ROWS == 0
    in_specs = [pl.BlockSpec((tm, K), lambda j, i: (i, 0))]
    args = [a]
    for seg in segs:
        assert seg.n_cols == steps * seg.tn and seg.col_off % seg.tn == 0 and seg.tn % MXU_COLS == 0
        in_specs.append(pl.BlockSpec((K, seg.tn), lambda j, i, off=seg.col_off // seg.tn: (0, j + off)))
        args.append(w)
    if res is not None:
        assert len(segs) == 1 and segs[0].mode == "res"
        in_specs.append(pl.BlockSpec((tm, segs[0].tn), lambda j, i: (i, j)))
        args.append(res)
    if rope is not None:
        (seg,) = [s for s in segs if s.mode == "rope"]
        P = rope[0].shape[0]
        assert P % tm == 0 and seg.n_q_cols % seg.tn == 0
        nper = P // tm
        for t in rope:
            in_specs.append(pl.BlockSpec((tm, LANES), lambda j, i: (i % nper, 0)))
            args.append(t)
    return pl.pallas_call(
        functools.partial(_proj_kernel, segs=tuple(segs), has_res=res is not None, has_rope=rope is not None),
        grid=(steps, M // tm),
        in_specs=in_specs,
        out_specs=[pl.BlockSpec((tm, seg.tn), lambda j, i: (i, j)) for seg in segs],
        out_shape=[jax.ShapeDtypeStruct((M, seg.n_cols), seg.out_dtype) for seg in segs],
        scratch_shapes=[pltpu.VMEM((K, seg.tn), BF16) for seg in segs],
        compiler_params=_params("arbitrary", "arbitrary"),
        name=name,
    )(*args)


NA_QROWS = 4
NA_KROWS = 12
NA_UNROLL = 8


def _na_block_tables(rows):
    nblk = rows // NA_QROWS
    kb0s, vids, variants = [], [], []
    for i in range(nblk):
        kb0 = min(max(NA_QROWS * i - NA_KH // 2, 0), rows - NA_KROWS)
        tab = []
        for a in range(NA_QROWS):
            r = NA_QROWS * i + a
            rs = min(max(r - NA_KH // 2, 0), rows - NA_KH)
            for kl in range(NA_KROWS):
                kr = kb0 + kl
                tab.append(kr - r + NA_KH - 1 if rs <= kr < rs + NA_KH else None)
        tab = tuple(tab)
        if tab not in variants:
            variants.append(tab)
        kb0s.append(kb0)
        vids.append(variants.index(tab))
    return kb0s, vids, variants


def _na_kernel(rpb_ref, kb0_ref, vid_ref, q_ref, k_ref, v_ref, g_ref, o_ref, tile_ref, bias_ref,
               *, variants, nblk):
    h = pl.program_id(0)
    W = GRID_W
    n_dr, n_dc = 2 * NA_KH - 1, 2 * NA_KW - 1

    @pl.when(pl.program_id(1) == 0)
    def _build_bias():
        qc = lax.broadcasted_iota(jnp.int32, (W, LANES), 0)
        lane = lax.broadcasted_iota(jnp.int32, (W, LANES), 1)
        kc = lane & (W - 1)
        cs = jnp.clip(qc - NA_KW // 2, 0, W - NA_KW)
        col_ok = (kc >= cs) & (kc < cs + NA_KW)
        dc = jnp.clip(kc - qc + NA_KW - 1, 0, n_dc - 1)

        def build_tile(dr, carry):
            t = jnp.zeros((W, LANES), F32)
            base = (h * n_dr + dr) * n_dc
            for j in range(n_dc):
                t = jnp.where(dc == j, rpb_ref[base + j] * LOG2E, t)
            tile_ref[dr] = jnp.where(col_ok, t, NEG_INF)
            return carry

        lax.fori_loop(0, n_dr, build_tile, 0)
        neg = jnp.full((W, LANES), NEG_INF, F32)
        for vi, tab in enumerate(variants):
            for a in range(NA_QROWS):
                for p in range(NA_KROWS // 2):
                    d0 = tab[a * NA_KROWS + 2 * p]
                    d1 = tab[a * NA_KROWS + 2 * p + 1]
                    t0 = neg if d0 is None else tile_ref[d0]
                    t1 = neg if d1 is None else tile_ref[d1]
                    bias_ref[vi, a * W:(a + 1) * W, p * LANES:(p + 1) * LANES] = jnp.where(lane < W, t0, t1)

    nq, nk = NA_QROWS * W, NA_KROWS * W

    def load(i):
        q0 = pl.multiple_of(i * nq, nq)
        k0 = pl.multiple_of(kb0_ref[i] * W, 2 * LANES)
        return (q_ref[0, pl.ds(q0, nq), :], k_ref[0, pl.ds(k0, nk), :], v_ref[0, pl.ds(k0, nk), :],
                bias_ref[vid_ref[i]])

    def group(n, carry):
        blocks = [n * NA_UNROLL + u for u in range(NA_UNROLL)]
        for i, (o, _, _) in zip(blocks, _attend([load(i) for i in blocks])):
            q0 = pl.multiple_of(i * nq, nq)
            g = g_ref[0, pl.ds(q0, nq), :]
            o_ref[0, pl.ds(q0, nq), :] = (o * (g / (1.0 + jnp.exp(-g)))).astype(o_ref.dtype)
        return carry

    lax.fori_loop(0, nblk // NA_UNROLL, group, 0)


def _na_attention(qkv, gate, rpb, *, heads):
    B, S, _ = qkv.shape
    rows = S // GRID_W
    assert rows % (NA_UNROLL * NA_QROWS) == 0 and rows >= NA_KROWS
    kb0s, vids, variants = _na_block_tables(rows)
    nblk = rows // NA_QROWS
    smem = pl.BlockSpec(memory_space=pltpu.SMEM)
    tok = lambda off: pl.BlockSpec((1, S, HEAD_DIM), lambda h, b: (b, 0, h + off))
    return pl.pallas_call(
        functools.partial(_na_kernel, variants=variants, nblk=nblk),
        grid=(heads, B),
        in_specs=[smem, smem, smem, tok(0), tok(heads), tok(2 * heads), tok(0)],
        out_specs=tok(0),
        out_shape=jax.ShapeDtypeStruct((B, S, heads * HEAD_DIM), BF16),
        scratch_shapes=[pltpu.VMEM((2 * NA_KH - 1, GRID_W, LANES), F32),
                        pltpu.VMEM((len(variants), NA_QROWS * GRID_W, NA_KROWS * GRID_W), F32)],
        compiler_params=_params("arbitrary", "arbitrary"),
        name="na_attention",
    )(rpb.reshape(-1), jnp.asarray(kb0s, jnp.int32), jnp.asarray(vids, jnp.int32), qkv, qkv, qkv, gate)


DL_QB = 256
DL_UNROLL = 8


def _dilated_kernel(q_ref, k_ref, v_ref, o_ref, lse_ref, mask_ref, *, dil, half, L):
    h = pl.program_id(1)
    qb = min(DL_QB, L - 2 * half)
    kb = qb + 2 * half
    nqb = L // qb

    @pl.when(h == 0)
    def _():
        lse_ref[...] = jnp.zeros_like(lse_ref)

    rel = (lax.broadcasted_iota(jnp.int32, (qb, kb), 1) - lax.broadcasted_iota(jnp.int32, (qb, kb), 0))
    for vi in range(3):
        mask_ref[vi] = jnp.where(jnp.abs(rel - vi * half) <= half, 0.0, NEG_INF)
    lane = lax.broadcasted_iota(jnp.int32, (qb, LANES), 1)

    def load(n):
        r = n // nqb
        q0 = pl.multiple_of((n % nqb) * qb, qb)
        k0 = pl.multiple_of(jnp.clip(q0 - half, 0, L - kb), half)
        return (q_ref[0, r, pl.ds(q0, qb), :], k_ref[0, r, pl.ds(k0, kb), :], v_ref[0, r, pl.ds(k0, kb), :],
                mask_ref[(q0 - k0) // half])

    unroll = math.gcd(dil * nqb, DL_UNROLL)

    def group(t, carry):
        blocks = [t * unroll + u for u in range(unroll)]
        for n, (o, m, den) in zip(blocks, _attend([load(n) for n in blocks])):
            r = n // nqb
            q0 = pl.multiple_of((n % nqb) * qb, qb)
            lse = m * LN2 + jnp.log(den)
            if dil == 1:
                rows = pl.ds(q0, qb)
            else:
                rows = pl.ds(q0 * dil + r, qb, stride=dil)
            o_ref[0, rows, :] = o
            lse_ref[0, rows, :] = jnp.where(lane == h, lse, lse_ref[0, rows, :])
        return carry

    lax.fori_loop(0, dil * nqb // unroll, group, 0)


def _dilated_attention(qk, v, *, B, dil, half, heads):
    L = qk.shape[2]
    S = L * dil
    qb = min(DL_QB, L - 2 * half)
    assert L % qb == 0 and heads <= LANES
    blk = lambda off: pl.BlockSpec((1, dil, L, HEAD_DIM), lambda b, h: (b, 0, 0, h + off))
    return pl.pallas_call(
        functools.partial(_dilated_kernel, dil=dil, half=half, L=L),
        grid=(B, heads),
        in_specs=[blk(0), blk(heads), blk(0)],
        out_specs=[pl.BlockSpec((1, S, HEAD_DIM), lambda b, h: (b, 0, h)),
                   pl.BlockSpec((1, S, LANES), lambda b, h: (b, 0, 0))],
        out_shape=[jax.ShapeDtypeStruct((B, S, heads * HEAD_DIM), F32),
                   jax.ShapeDtypeStruct((B, S, LANES), F32)],
        scratch_shapes=[pltpu.VMEM((3, qb, qb + 2 * half), F32)],
        compiler_params=_params("arbitrary", "arbitrary"),
        name=f"dilated_attention_d{dil}",
    )(qk, qk, v)


def _mix_out_kernel(*refs, n_groups, heads):
    o_refs = refs[:n_groups]
    l_refs = refs[n_groups:2 * n_groups]
    g_ref, w_ref, x_ref, nw_ref, out_ref, ya_ref, yb_ref = refs[2 * n_groups:]
    s = pl.program_id(0)
    D = out_ref.shape[1]

    @pl.when(s == 0)
    def _():
        yb_ref[...] = jnp.zeros_like(yb_ref)

    def step(y_fill, y_use):
        lses = [r[...] for r in l_refs]
        mx = functools.reduce(jnp.maximum, lses)
        es = [jnp.exp(l - mx) for l in lses]
        inv = 1.0 / functools.reduce(lambda a, b: a + b, es)
        wts = [e * inv for e in es]
        for h in range(heads):
            sl = slice(h * HEAD_DIM, (h + 1) * HEAD_DIM)
            o = None
            for wt, o_ref in zip(wts, o_refs):
                t = wt[:, h:h + 1] * o_ref[:, sl]
                o = t if o is None else o + t
            g = g_ref[:, sl]
            y_fill[:, sl] = (o * (g / (1.0 + jnp.exp(-g)))).astype(y_fill.dtype)

        ssq = None
        for c in range(D // MXU_COLS):
            cols = slice(c * MXU_COLS, (c + 1) * MXU_COLS)
            x = x_ref[:, cols] + jnp.dot(y_use[...], w_ref[:, cols], preferred_element_type=F32)
            out_ref[:, cols] = x
            sq = x * x
            for hh in range(MXU_COLS // LANES):
                part = sq[:, hh * LANES:(hh + 1) * LANES]
                ssq = part if ssq is None else ssq + part
        scale = lax.rsqrt(jnp.sum(ssq, axis=-1, keepdims=True) * (1.0 / D) + RMS_EPS)
        out_ref[...] = out_ref[...] * scale * nw_ref[...]

    @pl.when(s % 2 == 0)
    def _():
        step(ya_ref, yb_ref)

    @pl.when(s % 2 == 1)
    def _():
        step(yb_ref, ya_ref)


def _mix_out(os_, lses, gate, w, x, norm_w, *, heads, tm):
    M, D = x.shape
    width = heads * HEAD_DIM
    n = len(os_)
    n_tiles = M // tm
    fill = lambda c: pl.BlockSpec((tm, c), lambda s: (jnp.minimum(s, n_tiles - 1), 0))
    use = lambda c: pl.BlockSpec((tm, c), lambda s: (jnp.maximum(s - 1, 0), 0))
    return pl.pallas_call(
        functools.partial(_mix_out_kernel, n_groups=n, heads=heads),
        grid=(n_tiles + 1,),
        in_specs=[fill(width)] * n + [fill(LANES)] * n + [
            fill(width),
            pl.BlockSpec((width, D), lambda s: (0, 0), pipeline_mode=pl.Buffered(1)),
            use(D),
            pl.BlockSpec((1, D), lambda s: (0, 0))],
        out_specs=use(D),
        out_shape=jax.ShapeDtypeStruct((M, D), F32),
        scratch_shapes=[pltpu.VMEM((tm, width), BF16), pltpu.VMEM((tm, width), BF16)],
        compiler_params=_params("arbitrary"),
        name="mix_out_norm",
    )(*os_, *lses, gate, w, x, norm_w.reshape(1, D))


def _rope_tables(S):
    half = ROPE_DIM // 2
    inv_freq = jnp.power(ROPE_THETA, -jnp.arange(half, dtype=F32) * (2.0 / ROPE_DIM))
    ang = jnp.arange(S).astype(F32)[:, None] * inv_freq[None, :]
    cos, sin = jnp.cos(ang), jnp.sin(ang)
    zeros = jnp.zeros((S, LANES - ROPE_DIM), F32)
    z16 = jnp.zeros((S, half), F32)
    return (jnp.concatenate([cos, cos, 1.0 + zeros], axis=1),
            jnp.concatenate([z16, sin, zeros], axis=1),
            jnp.concatenate([-sin, z16, zeros], axis=1))


def _deinterleave(t, dil):
    S, C = t.shape
    return t.reshape(S // dil, dil, C).transpose(1, 0, 2).reshape(S, C)


def _forward(x, norm_w, final_norm_w, na_w_in, na_rpb, na_w_out, dl_w_in, dl_w_out, *,
             tm_norm=256, tm=512, tm_plain=1024, tn_qkv0=768, tn=512, tm_out=256):
    B, S, D = x.shape
    M = B * S
    assert norm_w.shape[0] == 2 and na_w_in.shape[0] == 1 and dl_w_in.shape[0] == 1
    na_width = na_w_out.shape[1]
    na_heads = na_width // HEAD_DIM
    dl_width = dl_w_out.shape[1]
    dl_heads = dl_width // HEAD_DIM
    n_groups = len(DL_GROUPS)

    (xn0,) = _rmsnorm(x, norm_w[0], (1,), tm_norm)
    xn0 = xn0.reshape(M, D)
    (qkv0,) = _proj(xn0, na_w_in[0], [_Seg(0, 3 * na_width, tn_qkv0, BF16, n_q_cols=na_width)], tm=tm_plain,
                    name="proj0_qkv")
    (gate0,) = _proj(xn0, na_w_in[0], [_Seg(3 * na_width, na_width, tn, F32)], tm=tm_plain, name="proj0_gate")
    y0 = _na_attention(qkv0.reshape(B, S, 3 * na_width), gate0.reshape(B, S, na_width), na_rpb[0],
                       heads=na_heads)
    (x1,) = _proj(y0.reshape(M, na_width), na_w_out[0], [_Seg(0, D, tn, F32, mode="res")], tm=tm_plain,
                  res=x.reshape(M, D), name="out0")

    w_in1 = dl_w_in[0]
    dils = tuple(sorted({d for _, d in DL_GROUPS}))
    xn1 = dict(zip(dils, _rmsnorm(x1.reshape(B, S, D), norm_w[1], dils, tm_norm)))
    rope = _rope_tables(S)
    os_, lses = [], []
    for gi, (window, dil) in enumerate(DL_GROUPS):
        half = window // (2 * dil)
        off = gi * 3 * dl_width
        qk, v = _proj(xn1[dil].reshape(M, D), w_in1,
                      [_Seg(off, 2 * dl_width, tn, BF16, mode="rope", n_q_cols=dl_width),
                       _Seg(off + 2 * dl_width, dl_width, tn // 2, BF16)],
                      tm=min(tm, S // dil), rope=[_deinterleave(t, dil) for t in rope], name=f"proj1_g{gi}")
        o, lse = _dilated_attention(qk.reshape(B, dil, S // dil, 2 * dl_width),
                                    v.reshape(B, dil, S // dil, dl_width), B=B, dil=dil, half=half,
                                    heads=dl_heads)
        os_.append(o.reshape(M, dl_width))
        lses.append(lse.reshape(M, LANES))
    (gate1,) = _proj(xn1[1].reshape(M, D), w_in1, [_Seg(3 * n_groups * dl_width, dl_width, tn, F32)], tm=tm_plain,
                     name="proj1_gate")
    out = _mix_out(os_, lses, gate1, dl_w_out[0].astype(BF16), x1, final_norm_w, heads=dl_heads, tm=tm_out)
    return out.reshape(B, S, D)


def kernel(x, norm_w, final_norm_w, na_w_in, na_rpb, na_w_out, dl_w_in, dl_w_out):
    return _forward(x, norm_w, final_norm_w, na_w_in, na_rpb, na_w_out, dl_w_in, dl_w_out)
```

```python
import functools
import math
from typing import Any, NamedTuple

import jax
import jax.numpy as jnp
from jax import lax
from jax.experimental import pallas as pl
from jax.experimental.pallas import tpu as pltpu

HEAD_DIM = 128
GRID_W = 64
NA_KH = 8
NA_KW = 16
DL_GROUPS = ((128, 1), (512, 4), (2048, 16))
ROPE_THETA = 500000.0
ROPE_DIM = HEAD_DIM // 4
RMS_EPS = 1e-6
NEG_INF = -1e30
SCALE = HEAD_DIM ** -0.5
LOG2E = math.log2(math.e)
LN2 = math.log(2.0)
Q_SCALE = SCALE * LOG2E

LANES = 128
MXU_COLS = 256
VMEM_LIMIT_BYTES = 56 * 1024 * 1024

BF16 = jnp.bfloat16
F32 = jnp.float32


def _params(*semantics):
    return pltpu.CompilerParams(dimension_semantics=semantics, vmem_limit_bytes=VMEM_LIMIT_BYTES)


def _attend(blocks):
    logits = [lax.dot_general(q, k, (((1,), (1,)), ((), ())), preferred_element_type=F32) + bias
              for q, k, _, bias in blocks]
    out = []
    for s, (_, _, v, _) in zip(logits, blocks):
        m = jnp.max(s, axis=-1, keepdims=True)
        p = jnp.exp2(s - m).astype(BF16)
        v_aug = jnp.concatenate([v, jnp.ones_like(v)], axis=1)
        pv = jnp.dot(p, v_aug, preferred_element_type=F32)
        den = pv[:, HEAD_DIM:]
        out.append((pv[:, :HEAD_DIM] / den, m, den))
    return out


def _rmsnorm_kernel(x_ref, g_ref, *rest, dils):
    out_refs, y_ref = rest[:-1], rest[-1]
    x = x_ref[0]
    y = x * lax.rsqrt(jnp.mean(x * x, axis=-1, keepdims=True) + RMS_EPS)
    y = y * g_ref[...]
    tm, D = x.shape
    n_chunks = D // LANES
    if any(d > 1 for d in dils):
        for c in range(n_chunks):
            y_ref[c] = y[:, c * LANES:(c + 1) * LANES]
    for d, o_ref in zip(dils, out_refs):
        if d == 1:
            o_ref[0, 0] = y.astype(o_ref.dtype)
        else:
            for c in range(n_chunks):
                for r in range(d):
                    o_ref[0, r, :, c * LANES:(c + 1) * LANES] = (
                        y_ref[c, pl.ds(r, tm // d, stride=d), :].astype(o_ref.dtype))


def _rmsnorm(x, g, dils, tm):
    B, S, D = x.shape
    out_shape = [jax.ShapeDtypeStruct((B, d, S // d, D), BF16) for d in dils]
    out_specs = [pl.BlockSpec((1, d, tm // d, D), lambda b, i: (b, 0, i, 0)) for d in dils]
    return pl.pallas_call(
        functools.partial(_rmsnorm_kernel, dils=dils),
        grid=(B, S // tm),
        in_specs=[pl.BlockSpec((1, tm, D), lambda b, i: (b, i, 0)),
                  pl.BlockSpec((1, D), lambda b, i: (0, 0))],
        out_specs=out_specs,
        out_shape=out_shape,
        scratch_shapes=[pltpu.VMEM((D // LANES, tm, LANES), F32)],
        compiler_params=_params("parallel", "parallel"),
        name="rmsnorm",
    )(x, g.reshape(1, D))


CAST_ROWS = 256


class _Seg(NamedTuple):
    col_off: int
    n_cols: int
    tn: int
    out_dtype: Any
    mode: str = "plain"
    n_q_cols: int = 0


def _proj_kernel(a_ref, *refs, segs, has_res, has_rope):
    n = len(segs)
    w_refs, refs = refs[:n], refs[n:]
    if has_res:
        res_ref, refs = refs[0], refs[1:]
    if has_rope:
        (cos_ref, shi_ref, slo_ref), refs = refs[:3], refs[3:]
    o_refs, wb_refs = refs[:n], refs[n:]
    j = pl.program_id(0)

    @pl.when(pl.program_id(1) == 0)
    def _convert_weights():
        for w_ref, wb_ref in zip(w_refs, wb_refs):
            def body(r, carry, w_ref=w_ref, wb_ref=wb_ref):
                rows = pl.ds(pl.multiple_of(r * CAST_ROWS, CAST_ROWS), CAST_ROWS)
                wb_ref[rows, :] = w_ref[rows, :].astype(BF16)
                return carry
            lax.fori_loop(0, w_ref.shape[0] // CAST_ROWS, body, 0)

    half = ROPE_DIM // 2
    for seg, wb_ref, o_ref in zip(segs, wb_refs, o_refs):
        for c in range(seg.tn // MXU_COLS):
            cols = slice(c * MXU_COLS, (c + 1) * MXU_COLS)
            acc = jnp.dot(a_ref[...], wb_ref[:, cols], preferred_element_type=F32)
            if seg.mode == "res":
                acc = res_ref[:, cols] + acc
            elif seg.mode == "rope":
                fac = jnp.where(j * seg.tn < seg.n_q_cols, Q_SCALE, 1.0).astype(F32)
                heads = []
                for hh in range(MXU_COLS // LANES):
                    x = acc[:, hh * LANES:(hh + 1) * LANES] * fac
                    heads.append(x * cos_ref[...] + pltpu.roll(x, half, 1) * shi_ref[...]
                                 + pltpu.roll(x, LANES - half, 1) * slo_ref[...])
                acc = jnp.concatenate(heads, axis=1)
            elif seg.n_q_cols:
                acc = acc * jnp.where(j * seg.tn + c * MXU_COLS < seg.n_q_cols, Q_SCALE, 1.0).astype(F32)
            o_ref[:, cols] = acc.astype(o_ref.dtype)


def _proj(a, w, segs, *, tm, res=None, rope=None, name):
    M, K = a.shape
    steps = segs[0].n_cols // segs[0].tn
    assert M % tm == 0 and K % CAST_ROWS == 0
    in_specs = [pl.BlockSpec((tm, K), lambda j, i: (i, 0))]
    args = [a]
    for seg in segs:
        assert seg.n_cols == steps * seg.tn and seg.col_off % seg.tn == 0 and seg.tn % MXU_COLS == 0
        in_specs.append(pl.BlockSpec((K, seg.tn), lambda j, i, off=seg.col_off // seg.tn: (0, j + off)))
        args.append(w)
    if res is not None:
        assert len(segs) == 1 and segs[0].mode == "res"
        in_specs.append(pl.BlockSpec((tm, segs[0].tn), lambda j, i: (i, j)))
        args.append(res)
    if rope is not None:
        (seg,) = [s for s in segs if s.mode == "rope"]
        P = rope[0].shape[0]
        assert P % tm == 0 and seg.n_q_cols % seg.tn == 0
        nper = P // tm
        for t in rope:
            in_specs.append(pl.BlockSpec((tm, LANES), lambda j, i: (i % nper, 0)))
            args.append(t)
    return pl.pallas_call(
        functools.partial(_proj_kernel, segs=tuple(segs), has_res=res is not None, has_rope=rope is not None),
        grid=(steps, M // tm),
        in_specs=in_specs,
        out_specs=[pl.BlockSpec((tm, seg.tn), lambda j, i: (i, j)) for seg in segs],
        out_shape=[jax.ShapeDtypeStruct((M, seg.n_cols), seg.out_dtype) for seg in segs],
        scratch_shapes=[pltpu.VMEM((K, seg.tn), BF16) for seg in segs],
        compiler_params=_params("arbitrary", "arbitrary"),
        name=name,
    )(*args)


NA_QROWS = 4
NA_KROWS = 12
NA_UNROLL = 8


def _na_block_tables(rows):
    nblk = rows // NA_QROWS
    kb0s, vids, variants = [], [], []
    for i in range(nblk):
        kb0 = min(max(NA_QROWS * i - NA_KH // 2, 0), rows - NA_KROWS)
        tab = []
        for a in range(NA_QROWS):
            r = NA_QROWS * i + a
            rs = min(max(r - NA_KH // 2, 0), rows - NA_KH)
            for kl in range(NA_KROWS):
                kr = kb0 + kl
                tab.append(kr - r + NA_KH - 1 if rs <= kr < rs + NA_KH else None)
        tab = tuple(tab)
        if tab not in variants:
            variants.append(tab)
        kb0s.append(kb0)
        vids.append(variants.index(tab))
    return kb0s, vids, variants


def _na_kernel(rpb_ref, kb0_ref, vid_ref, q_ref, k_ref, v_ref, g_ref, o_ref, tile_ref, bias_ref,
               *, variants, nblk):
    h = pl.program_id(0)
    W = GRID_W
    n_dr, n_dc = 2 * NA_KH - 1, 2 * NA_KW - 1

    @pl.when(pl.program_id(1) == 0)
    def _build_bias():
        qc = lax.broadcasted_iota(jnp.int32, (W, LANES), 0)
        lane = lax.broadcasted_iota(jnp.int32, (W, LANES), 1)
        kc = lane & (W - 1)
        cs = jnp.clip(qc - NA_KW // 2, 0, W - NA_KW)
        col_ok = (kc >= cs) & (kc < cs + NA_KW)
        dc = jnp.clip(kc - qc + NA_KW - 1, 0, n_dc - 1)

        def build_tile(dr, carry):
            t = jnp.zeros((W, LANES), F32)
            base = (h * n_dr + dr) * n_dc
            for j in range(n_dc):
                t = jnp.where(dc == j, rpb_ref[base + j] * LOG2E, t)
            tile_ref[dr] = jnp.where(col_ok, t, NEG_INF)
            return carry

        lax.fori_loop(0, n_dr, build_tile, 0)
        neg = jnp.full((W, LANES), NEG_INF, F32)
        for vi, tab in enumerate(variants):
            for a in range(NA_QROWS):
                for p in range(NA_KROWS // 2):
                    d0 = tab[a * NA_KROWS + 2 * p]
                    d1 = tab[a * NA_KROWS + 2 * p + 1]
                    t0 = neg if d0 is None else tile_ref[d0]
                    t1 = neg if d1 is None else tile_ref[d1]
                    bias_ref[vi, a * W:(a + 1) * W, p * LANES:(p + 1) * LANES] = jnp.where(lane < W, t0, t1)

    nq, nk = NA_QROWS * W, NA_KROWS * W

    def load(i):
        q0 = pl.multiple_of(i * nq, nq)
        k0 = pl.multiple_of(kb0_ref[i] * W, 2 * LANES)
        return (q_ref[0, pl.ds(q0, nq), :], k_ref[0, pl.ds(k0, nk), :], v_ref[0, pl.ds(k0, nk), :],
                bias_ref[vid_ref[i]])

    def group(n, carry):
        blocks = [n * NA_UNROLL + u for u in range(NA_UNROLL)]
        for i, (o, _, _) in zip(blocks, _attend([load(i) for i in blocks])):
            q0 = pl.multiple_of(i * nq, nq)
            g = g_ref[0, pl.ds(q0, nq), :]
            o_ref[0, pl.ds(q0, nq), :] = (o * (g / (1.0 + jnp.exp(-g)))).astype(o_ref.dtype)
        return carry

    lax.fori_loop(0, nblk // NA_UNROLL, group, 0)


def _na_attention(qkv, gate, rpb, *, heads):
    B, S, _ = qkv.shape
    rows = S // GRID_W
    assert rows % (NA_UNROLL * NA_QROWS) == 0 and rows >= NA_KROWS
    kb0s, vids, variants = _na_block_tables(rows)
    nblk = rows // NA_QROWS
    smem = pl.BlockSpec(memory_space=pltpu.SMEM)
    tok = lambda off: pl.BlockSpec((1, S, HEAD_DIM), lambda h, b: (b, 0, h + off))
    return pl.pallas_call(
        functools.partial(_na_kernel, variants=variants, nblk=nblk),
        grid=(heads, B),
        in_specs=[smem, smem, smem, tok(0), tok(heads), tok(2 * heads), tok(0)],
        out_specs=tok(0),
        out_shape=jax.ShapeDtypeStruct((B, S, heads * HEAD_DIM), BF16),
        scratch_shapes=[pltpu.VMEM((2 * NA_KH - 1, GRID_W, LANES), F32),
                        pltpu.VMEM((len(variants), NA_QROWS * GRID_W, NA_KROWS * GRID_W), F32)],
        compiler_params=_params("arbitrary", "arbitrary"),
        name="na_attention",
    )(rpb.reshape(-1), jnp.asarray(kb0s, jnp.int32), jnp.asarray(vids, jnp.int32), qkv, qkv, qkv, gate)


DL_QB = 128
DL_UNROLL = 16
DL_MAX_STRIDE = 4


def _dilated_kernel(q_ref, k_ref, v_ref, o_ref, lse_ref, mask_ref, *stage_refs, dil, half, L):
    h = pl.program_id(1)
    qb = min(DL_QB, L - 2 * half)
    kb = qb + 2 * half
    nqb = L // qb
    d_in = min(dil, DL_MAX_STRIDE)
    d_out = dil // d_in

    @pl.when(h == 0)
    def _():
        lse_ref[...] = jnp.zeros_like(lse_ref)

    rel = (lax.broadcasted_iota(jnp.int32, (qb, kb), 1) - lax.broadcasted_iota(jnp.int32, (qb, kb), 0))
    for vi in range(3):
        mask_ref[vi] = jnp.where(jnp.abs(rel - vi * half) <= half, 0.0, NEG_INF)

    def load(r, q0):
        k0 = pl.multiple_of(jnp.clip(q0 - half, 0, L - kb), half)
        return (q_ref[0, r, pl.ds(q0, qb), :], k_ref[0, r, pl.ds(k0, kb), :], v_ref[0, r, pl.ds(k0, kb), :],
                mask_ref[(q0 - k0) // half])

    def put_lse(rows, lse):
        lane = lax.broadcasted_iota(jnp.int32, lse.shape, 1)
        lse_ref[0, rows, :] = jnp.where(lane == h, lse, lse_ref[0, rows, :])

    if d_out == 1:
        unroll = math.gcd(dil * nqb, DL_UNROLL)

        def group(t, carry):
            blocks = [((t * unroll + u) // nqb, pl.multiple_of(((t * unroll + u) % nqb) * qb, qb))
                      for u in range(unroll)]
            for (r, q0), (o, m, den) in zip(blocks, _attend([load(r, q0) for r, q0 in blocks])):
                rows = pl.ds(q0, qb) if dil == 1 else pl.ds(q0 * dil + r, qb, stride=dil)
                o_ref[0, rows, :] = o
                put_lse(rows, m * LN2 + jnp.log(den))
            return carry

        lax.fori_loop(0, dil * nqb // unroll, group, 0)
    else:
        o_tmp, l_tmp = stage_refs

        def group(r0, carry):
            blocks = [(qi, r1) for qi in range(nqb) for r1 in range(d_out)]
            outs = _attend([load(r1 * d_in + r0, qi * qb) for qi, r1 in blocks])
            for (qi, r1), (o, m, den) in zip(blocks, outs):
                hop1 = pl.ds(r1, qb, stride=d_out)
                o_tmp[qi, hop1, :] = o
                l_tmp[qi, hop1, :] = m * LN2 + jnp.log(den)
            for qi in range(nqb):
                hop2 = pl.ds(qi * qb * dil + r0, qb * d_out, stride=d_in)
                o_ref[0, hop2, :] = o_tmp[qi]
                put_lse(hop2, l_tmp[qi])
            return carry

        lax.fori_loop(0, d_in, group, 0)


def _dilated_attention(qk, v, *, B, dil, half, heads):
    L = qk.shape[2]
    S = L * dil
    qb = min(DL_QB, L - 2 * half)
    assert L % qb == 0 and heads <= LANES
    d_out = dil // min(dil, DL_MAX_STRIDE)
    assert dil % min(dil, DL_MAX_STRIDE) == 0
    scratch = [pltpu.VMEM((3, qb, qb + 2 * half), F32)]
    if d_out > 1:
        scratch += [pltpu.VMEM((L // qb, qb * d_out, LANES), F32)] * 2
    blk = lambda off: pl.BlockSpec((1, dil, L, HEAD_DIM), lambda b, h: (b, 0, 0, h + off))
    return pl.pallas_call(
        functools.partial(_dilated_kernel, dil=dil, half=half, L=L),
        grid=(B, heads),
        in_specs=[blk(0), blk(heads), blk(0)],
        out_specs=[pl.BlockSpec((1, S, HEAD_DIM), lambda b, h: (b, 0, h)),
                   pl.BlockSpec((1, S, LANES), lambda b, h: (b, 0, 0))],
        out_shape=[jax.ShapeDtypeStruct((B, S, heads * HEAD_DIM), F32),
                   jax.ShapeDtypeStruct((B, S, LANES), F32)],
        scratch_shapes=scratch,
        compiler_params=_params("arbitrary", "arbitrary"),
        name=f"dilated_attention_d{dil}",
    )(qk, qk, v)


def _mix_out_kernel(*refs, n_groups, heads):
    o_refs = refs[:n_groups]
    l_refs = refs[n_groups:2 * n_groups]
    g_ref, w_ref, x_ref, nw_ref, out_ref, ya_ref, yb_ref = refs[2 * n_groups:]
    s = pl.program_id(0)
    D = out_ref.shape[1]

    @pl.when(s == 0)
    def _():
        yb_ref[...] = jnp.zeros_like(yb_ref)

    def step(y_fill, y_use):
        lses = [r[...] for r in l_refs]
        mx = functools.reduce(jnp.maximum, lses)
        es = [jnp.exp(l - mx) for l in lses]
        inv = 1.0 / functools.reduce(lambda a, b: a + b, es)
        wts = [e * inv for e in es]
        for h in range(heads):
            sl = slice(h * HEAD_DIM, (h + 1) * HEAD_DIM)
            o = None
            for wt, o_ref in zip(wts, o_refs):
                t = wt[:, h:h + 1] * o_ref[:, sl]
                o = t if o is None else o + t
            g = g_ref[:, sl]
            y_fill[:, sl] = (o * (g / (1.0 + jnp.exp(-g)))).astype(y_fill.dtype)

        ssq = None
        for c in range(D // MXU_COLS):
            cols = slice(c * MXU_COLS, (c + 1) * MXU_COLS)
            x = x_ref[:, cols] + jnp.dot(y_use[...], w_ref[:, cols], preferred_element_type=F32)
            out_ref[:, cols] = x
            sq = x * x
            for hh in range(MXU_COLS // LANES):
                part = sq[:, hh * LANES:(hh + 1) * LANES]
                ssq = part if ssq is None else ssq + part
        scale = lax.rsqrt(jnp.sum(ssq, axis=-1, keepdims=True) * (1.0 / D) + RMS_EPS)
        out_ref[...] = out_ref[...] * scale * nw_ref[...]

    @pl.when(s % 2 == 0)
    def _():
        step(ya_ref, yb_ref)

    @pl.when(s % 2 == 1)
    def _():
        step(yb_ref, ya_ref)


def _mix_out(os_, lses, gate, w, x, norm_w, *, heads, tm):
    M, D = x.shape
    width = heads * HEAD_DIM
    n = len(os_)
    n_tiles = M // tm
    fill = lambda c: pl.BlockSpec((tm, c), lambda s: (jnp.minimum(s, n_tiles - 1), 0))
    use = lambda c: pl.BlockSpec((tm, c), lambda s: (jnp.maximum(s - 1, 0), 0))
    return pl.pallas_call(
        functools.partial(_mix_out_kernel, n_groups=n, heads=heads),
        grid=(n_tiles + 1,),
        in_specs=[fill(width)] * n + [fill(LANES)] * n + [
            fill(width),
            pl.BlockSpec((width, D), lambda s: (0, 0), pipeline_mode=pl.Buffered(1)),
            use(D),
            pl.BlockSpec((1, D), lambda s: (0, 0))],
        out_specs=use(D),
        out_shape=jax.ShapeDtypeStruct((M, D), F32),
        scratch_shapes=[pltpu.VMEM((tm, width), BF16), pltpu.VMEM((tm, width), BF16)],
        compiler_params=_params("arbitrary"),
        name="mix_out_norm",
    )(*os_, *lses, gate, w, x, norm_w.reshape(1, D))


def _rope_tables(S):
    half = ROPE_DIM // 2
    inv_freq = jnp.power(ROPE_THETA, -jnp.arange(half, dtype=F32) * (2.0 / ROPE_DIM))
    ang = jnp.arange(S).astype(F32)[:, None] * inv_freq[None, :]
    cos, sin = jnp.cos(ang), jnp.sin(ang)
    zeros = jnp.zeros((S, LANES - ROPE_DIM), F32)
    z16 = jnp.zeros((S, half), F32)
    return (jnp.concatenate([cos, cos, 1.0 + zeros], axis=1),
            jnp.concatenate([z16, sin, zeros], axis=1),
            jnp.concatenate([-sin, z16, zeros], axis=1))


def _deinterleave(t, dil):
    S, C = t.shape
    return t.reshape(S // dil, dil, C).transpose(1, 0, 2).reshape(S, C)


def _forward(x, norm_w, final_norm_w, na_w_in, na_rpb, na_w_out, dl_w_in, dl_w_out, *,
             tm_norm=256, tm=512, tm_plain=1024, tn_qkv0=768, tn=512, tm_out=256):
    B, S, D = x.shape
    M = B * S
    assert norm_w.shape[0] == 2 and na_w_in.shape[0] == 1 and dl_w_in.shape[0] == 1
    na_width = na_w_out.shape[1]
    na_heads = na_width // HEAD_DIM
    dl_width = dl_w_out.shape[1]
    dl_heads = dl_width // HEAD_DIM
    n_groups = len(DL_GROUPS)

    (xn0,) = _rmsnorm(x, norm_w[0], (1,), tm_norm)
    xn0 = xn0.reshape(M, D)
    (qkv0,) = _proj(xn0, na_w_in[0], [_Seg(0, 3 * na_width, tn_qkv0, BF16, n_q_cols=na_width)], tm=tm_plain,
                    name="proj0_qkv")
    (gate0,) = _proj(xn0, na_w_in[0], [_Seg(3 * na_width, na_width, tn, F32)], tm=tm_plain, name="proj0_gate")
    y0 = _na_attention(qkv0.reshape(B, S, 3 * na_width), gate0.reshape(B, S, na_width), na_rpb[0],
                       heads=na_heads)
    (x1,) = _proj(y0.reshape(M, na_width), na_w_out[0], [_Seg(0, D, tn, F32, mode="res")], tm=tm_plain,
                  res=x.reshape(M, D), name="out0")

    w_in1 = dl_w_in[0]
    dils = tuple(sorted({d for _, d in DL_GROUPS}))
    xn1 = dict(zip(dils, _rmsnorm(x1.reshape(B, S, D), norm_w[1], dils, tm_norm)))
    rope = _rope_tables(S)
    os_, lses = [], []
    for gi, (window, dil) in enumerate(DL_GROUPS):
        half = window // (2 * dil)
        off = gi * 3 * dl_width
        qk, v = _proj(xn1[dil].reshape(M, D), w_in1,
                      [_Seg(off, 2 * dl_width, tn, BF16, mode="rope", n_q_cols=dl_width),
                       _Seg(off + 2 * dl_width, dl_width, tn // 2, BF16)],
                      tm=min(tm, S // dil), rope=[_deinterleave(t, dil) for t in rope], name=f"proj1_g{gi}")
        o, lse = _dilated_attention(qk.reshape(B, dil, S // dil, 2 * dl_width),
                                    v.reshape(B, dil, S // dil, dl_width), B=B, dil=dil, half=half,
                                    heads=dl_heads)
        os_.append(o.reshape(M, dl_width))
        lses.append(lse.reshape(M, LANES))
    (gate1,) = _proj(xn1[1].reshape(M, D), w_in1, [_Seg(3 * n_groups * dl_width, dl_width, tn, F32)], tm=tm_plain,
                     name="proj1_gate")
    out = _mix_out(os_, lses, gate1, dl_w_out[0].astype(BF16), x1, final_norm_w, heads=dl_heads, tm=tm_out)
    return out.reshape(B, S, D)


def kernel(x, norm_w, final_norm_w, na_w_in, na_rpb, na_w_out, dl_w_in, dl_w_out):
    return _forward(x, norm_w, final_norm_w, na_w_in, na_rpb, na_w_out, dl_w_in, dl_w_out)
```

```python
import functools
import math
from typing import Any, NamedTuple

import jax
import jax.numpy as jnp
from jax import lax
from jax.experimental import pallas as pl
from jax.experimental.pallas import tpu as pltpu

HEAD_DIM = 128
GRID_W = 64
NA_KH = 8
NA_KW = 16
DL_GROUPS = ((128, 1), (512, 4), (2048, 16))
ROPE_THETA = 500000.0
ROPE_DIM = HEAD_DIM // 4
RMS_EPS = 1e-6
NEG_INF = -1e30
SCALE = HEAD_DIM ** -0.5
LOG2E = math.log2(math.e)
LN2 = math.log(2.0)
Q_SCALE = SCALE * LOG2E

LANES = 128
MXU_COLS = 256
VMEM_LIMIT_BYTES = 56 * 1024 * 1024

BF16 = jnp.bfloat16
F32 = jnp.float32


def _params(*semantics):
    return pltpu.CompilerParams(dimension_semantics=semantics, vmem_limit_bytes=VMEM_LIMIT_BYTES)


def _attend(blocks):
    logits = [lax.dot_general(q, k, (((1,), (1,)), ((), ())), preferred_element_type=F32) + bias
              for q, k, _, bias in blocks]
    out = []
    for s, (_, _, v, _) in zip(logits, blocks):
        m = jnp.max(s, axis=-1, keepdims=True)
        p = jnp.exp2(s - m).astype(BF16)
        v_aug = jnp.concatenate([v, jnp.ones_like(v)], axis=1)
        pv = jnp.dot(p, v_aug, preferred_element_type=F32)
        den = pv[:, HEAD_DIM:]
        out.append((pv[:, :HEAD_DIM] / den, m, den))
    return out


def _rmsnorm_kernel(x_ref, g_ref, *rest, dils):
    out_refs, y_ref = rest[:-1], rest[-1]
    x = x_ref[0]
    y = x * lax.rsqrt(jnp.mean(x * x, axis=-1, keepdims=True) + RMS_EPS)
    y = y * g_ref[...]
    tm, D = x.shape
    n_chunks = D // LANES
    if any(d > 1 for d in dils):
        for c in range(n_chunks):
            y_ref[c] = y[:, c * LANES:(c + 1) * LANES]
    for d, o_ref in zip(dils, out_refs):
        if d == 1:
            o_ref[0, 0] = y.astype(o_ref.dtype)
        else:
            for c in range(n_chunks):
                for r in range(d):
                    o_ref[0, r, :, c * LANES:(c + 1) * LANES] = (
                        y_ref[c, pl.ds(r, tm // d, stride=d), :].astype(o_ref.dtype))


def _rmsnorm(x, g, dils, tm):
    B, S, D = x.shape
    out_shape = [jax.ShapeDtypeStruct((B, d, S // d, D), BF16) for d in dils]
    out_specs = [pl.BlockSpec((1, d, tm // d, D), lambda b, i: (b, 0, i, 0)) for d in dils]
    return pl.pallas_call(
        functools.partial(_rmsnorm_kernel, dils=dils),
        grid=(B, S // tm),
        in_specs=[pl.BlockSpec((1, tm, D), lambda b, i: (b, i, 0)),
                  pl.BlockSpec((1, D), lambda b, i: (0, 0))],
        out_specs=out_specs,
        out_shape=out_shape,
        scratch_shapes=[pltpu.VMEM((D // LANES, tm, LANES), F32)],
        compiler_params=_params("parallel", "parallel"),
        name="rmsnorm",
    )(x, g.reshape(1, D))


CAST_ROWS = 256


class _Seg(NamedTuple):
    col_off: int
    n_cols: int
    tn: int
    out_dtype: Any
    mode: str = "plain"
    n_q_cols: int = 0


def _proj_kernel(a_ref, *refs, segs, has_res, has_rope):
    n = len(segs)
    w_refs, refs = refs[:n], refs[n:]
    if has_res:
        res_ref, refs = refs[0], refs[1:]
    if has_rope:
        (cos_ref, shi_ref, slo_ref), refs = refs[:3], refs[3:]
    o_refs, wb_refs = refs[:n], refs[n:]
    j = pl.program_id(0)

    @pl.when(pl.program_id(1) == 0)
    def _convert_weights():
        for w_ref, wb_ref in zip(w_refs, wb_refs):
            def body(r, carry, w_ref=w_ref, wb_ref=wb_ref):
                rows = pl.ds(pl.multiple_of(r * CAST_ROWS, CAST_ROWS), CAST_ROWS)
                wb_ref[rows, :] = w_ref[rows, :].astype(BF16)
                return carry
            lax.fori_loop(0, w_ref.shape[0] // CAST_ROWS, body, 0)

    half = ROPE_DIM // 2
    for seg, wb_ref, o_ref in zip(segs, wb_refs, o_refs):
        for c in range(seg.tn // MXU_COLS):
            cols = slice(c * MXU_COLS, (c + 1) * MXU_COLS)
            hm = a_ref.shape[0] // 2
            acc = jnp.concatenate(
                [jnp.dot(a_ref[0:hm, :], wb_ref[:, cols], preferred_element_type=F32),
                 jnp.dot(a_ref[hm:, :], wb_ref[:, cols], preferred_element_type=F32)], axis=0)
            if seg.mode == "res":
                acc = res_ref[:, cols] + acc
            elif seg.mode == "rope":
                fac = jnp.where(j * seg.tn < seg.n_q_cols, Q_SCALE, 1.0).astype(F32)
                heads = []
                for hh in range(MXU_COLS // LANES):
                    x = acc[:, hh * LANES:(hh + 1) * LANES] * fac
                    heads.append(x * cos_ref[...] + pltpu.roll(x, half, 1) * shi_ref[...]
                                 + pltpu.roll(x, LANES - half, 1) * slo_ref[...])
                acc = jnp.concatenate(heads, axis=1)
            elif seg.n_q_cols:
                acc = acc * jnp.where(j * seg.tn + c * MXU_COLS < seg.n_q_cols, Q_SCALE, 1.0).astype(F32)
            o_ref[:, cols] = acc.astype(o_ref.dtype)


def _proj(a, w, segs, *, tm, res=None, rope=None, name):
    M, K = a.shape
    steps = segs[0].n_cols // segs[0].tn
    assert M % tm == 0 and K % CAST_ROWS == 0
    in_specs = [pl.BlockSpec((tm, K), lambda j, i: (i, 0))]
    args = [a]
    for seg in segs:
        assert seg.n_cols == steps * seg.tn and seg.col_off % seg.tn == 0 and seg.tn % MXU_COLS == 0
        in_specs.append(pl.BlockSpec((K, seg.tn), lambda j, i, off=seg.col_off // seg.tn: (0, j + off)))
        args.append(w)
    if res is not None:
        assert len(segs) == 1 and segs[0].mode == "res"
        in_specs.append(pl.BlockSpec((tm, segs[0].tn), lambda j, i: (i, j)))
        args.append(res)
    if rope is not None:
        (seg,) = [s for s in segs if s.mode == "rope"]
        P = rope[0].shape[0]
        assert P % tm == 0 and seg.n_q_cols % seg.tn == 0
        nper = P // tm
        for t in rope:
            in_specs.append(pl.BlockSpec((tm, LANES), lambda j, i: (i % nper, 0)))
            args.append(t)
    return pl.pallas_call(
        functools.partial(_proj_kernel, segs=tuple(segs), has_res=res is not None, has_rope=rope is not None),
        grid=(steps, M // tm),
        in_specs=in_specs,
        out_specs=[pl.BlockSpec((tm, seg.tn), lambda j, i: (i, j)) for seg in segs],
        out_shape=[jax.ShapeDtypeStruct((M, seg.n_cols), seg.out_dtype) for seg in segs],
        scratch_shapes=[pltpu.VMEM((K, seg.tn), BF16) for seg in segs],
        compiler_params=_params("arbitrary", "arbitrary"),
        name=name,
    )(*args)


NA_QROWS = 4
NA_KROWS = 12
NA_UNROLL = 8


def _na_block_tables(rows):
    nblk = rows // NA_QROWS
    kb0s, vids, variants = [], [], []
    for i in range(nblk):
        kb0 = min(max(NA_QROWS * i - NA_KH // 2, 0), rows - NA_KROWS)
        tab = []
        for a in range(NA_QROWS):
            r = NA_QROWS * i + a
            rs = min(max(r - NA_KH // 2, 0), rows - NA_KH)
            for kl in range(NA_KROWS):
                kr = kb0 + kl
                tab.append(kr - r + NA_KH - 1 if rs <= kr < rs + NA_KH else None)
        tab = tuple(tab)
        if tab not in variants:
            variants.append(tab)
        kb0s.append(kb0)
        vids.append(variants.index(tab))
    return kb0s, vids, variants


def _na_kernel(rpb_ref, kb0_ref, vid_ref, q_ref, k_ref, v_ref, g_ref, o_ref, tile_ref, bias_ref,
               *, variants, nblk):
    h = pl.program_id(0)
    W = GRID_W
    n_dr, n_dc = 2 * NA_KH - 1, 2 * NA_KW - 1

    @pl.when(pl.program_id(1) == 0)
    def _build_bias():
        qc = lax.broadcasted_iota(jnp.int32, (W, LANES), 0)
        lane = lax.broadcasted_iota(jnp.int32, (W, LANES), 1)
        kc = lane & (W - 1)
        cs = jnp.clip(qc - NA_KW // 2, 0, W - NA_KW)
        col_ok = (kc >= cs) & (kc < cs + NA_KW)
        dc = jnp.clip(kc - qc + NA_KW - 1, 0, n_dc - 1)

        def build_tile(dr, carry):
            t = jnp.zeros((W, LANES), F32)
            base = (h * n_dr + dr) * n_dc
            for j in range(n_dc):
                t = jnp.where(dc == j, rpb_ref[base + j] * LOG2E, t)
            tile_ref[dr] = jnp.where(col_ok, t, NEG_INF)
            return carry

        lax.fori_loop(0, n_dr, build_tile, 0)
        neg = jnp.full((W, LANES), NEG_INF, F32)
        for vi, tab in enumerate(variants):
            for a in range(NA_QROWS):
                for p in range(NA_KROWS // 2):
                    d0 = tab[a * NA_KROWS + 2 * p]
                    d1 = tab[a * NA_KROWS + 2 * p + 1]
                    t0 = neg if d0 is None else tile_ref[d0]
                    t1 = neg if d1 is None else tile_ref[d1]
                    bias_ref[vi, a * W:(a + 1) * W, p * LANES:(p + 1) * LANES] = jnp.where(lane < W, t0, t1)

    nq, nk = NA_QROWS * W, NA_KROWS * W

    def load(i):
        q0 = pl.multiple_of(i * nq, nq)
        k0 = pl.multiple_of(kb0_ref[i] * W, 2 * LANES)
        return (q_ref[0, pl.ds(q0, nq), :], k_ref[0, pl.ds(k0, nk), :], v_ref[0, pl.ds(k0, nk), :],
                bias_ref[vid_ref[i]])

    def group(n, carry):
        blocks = [n * NA_UNROLL + u for u in range(NA_UNROLL)]
        for i, (o, _, _) in zip(blocks, _attend([load(i) for i in blocks])):
            q0 = pl.multiple_of(i * nq, nq)
            g = g_ref[0, pl.ds(q0, nq), :]
            o_ref[0, pl.ds(q0, nq), :] = (o * (g / (1.0 + jnp.exp(-g)))).astype(o_ref.dtype)
        return carry

    lax.fori_loop(0, nblk // NA_UNROLL, group, 0)


def _na_attention(qkv, gate, rpb, *, heads):
    B, S, _ = qkv.shape
    rows = S // GRID_W
    assert rows % (NA_UNROLL * NA_QROWS) == 0 and rows >= NA_KROWS
    kb0s, vids, variants = _na_block_tables(rows)
    nblk = rows // NA_QROWS
    smem = pl.BlockSpec(memory_space=pltpu.SMEM)
    tok = lambda off: pl.BlockSpec((1, S, HEAD_DIM), lambda h, b: (b, 0, h + off))
    return pl.pallas_call(
        functools.partial(_na_kernel, variants=variants, nblk=nblk),
        grid=(heads, B),
        in_specs=[smem, smem, smem, tok(0), tok(heads), tok(2 * heads), tok(0)],
        out_specs=tok(0),
        out_shape=jax.ShapeDtypeStruct((B, S, heads * HEAD_DIM), BF16),
        scratch_shapes=[pltpu.VMEM((2 * NA_KH - 1, GRID_W, LANES), F32),
                        pltpu.VMEM((len(variants), NA_QROWS * GRID_W, NA_KROWS * GRID_W), F32)],
        compiler_params=_params("arbitrary", "arbitrary"),
        name="na_attention",
    )(rpb.reshape(-1), jnp.asarray(kb0s, jnp.int32), jnp.asarray(vids, jnp.int32), qkv, qkv, qkv, gate)


DL_QB = 128
DL_UNROLL = 16
DL_MAX_STRIDE = 4


def _dilated_kernel(q_ref, k_ref, v_ref, o_ref, lse_ref, mask_ref, *stage_refs, dil, half, L):
    h = pl.program_id(1)
    qb = min(DL_QB, L - 2 * half)
    kb = qb + 2 * half
    nqb = L // qb
    d_in = min(dil, DL_MAX_STRIDE)
    d_out = dil // d_in

    @pl.when(h == 0)
    def _():
        lse_ref[...] = jnp.zeros_like(lse_ref)

    rel = (lax.broadcasted_iota(jnp.int32, (qb, kb), 1) - lax.broadcasted_iota(jnp.int32, (qb, kb), 0))
    for vi in range(3):
        mask_ref[vi] = jnp.where(jnp.abs(rel - vi * half) <= half, 0.0, NEG_INF)

    def load(r, q0):
        k0 = pl.multiple_of(jnp.clip(q0 - half, 0, L - kb), half)
        return (q_ref[0, r, pl.ds(q0, qb), :], k_ref[0, r, pl.ds(k0, kb), :], v_ref[0, r, pl.ds(k0, kb), :],
                mask_ref[(q0 - k0) // half])

    def put_lse(rows, lse):
        lane = lax.broadcasted_iota(jnp.int32, lse.shape, 1)
        lse_ref[0, rows, :] = jnp.where(lane == h, lse, lse_ref[0, rows, :])

    if d_out == 1:
        unroll = math.gcd(dil * nqb, DL_UNROLL)

        def group(t, carry):
            blocks = [((t * unroll + u) // nqb, pl.multiple_of(((t * unroll + u) % nqb) * qb, qb))
                      for u in range(unroll)]
            for (r, q0), (o, m, den) in zip(blocks, _attend([load(r, q0) for r, q0 in blocks])):
                rows = pl.ds(q0, qb) if dil == 1 else pl.ds(q0 * dil + r, qb, stride=dil)
                o_ref[0, rows, :] = o
                put_lse(rows, m * LN2 + jnp.log(den))
            return carry

        lax.fori_loop(0, dil * nqb // unroll, group, 0)
    else:
        o_tmp, l_tmp = stage_refs

        def group(r0, carry):
            blocks = [(qi, r1) for qi in range(nqb) for r1 in range(d_out)]
            outs = _attend([load(r1 * d_in + r0, qi * qb) for qi, r1 in blocks])
            for (qi, r1), (o, m, den) in zip(blocks, outs):
                hop1 = pl.ds(r1, qb, stride=d_out)
                o_tmp[qi, hop1, :] = o
                l_tmp[qi, hop1, :] = m * LN2 + jnp.log(den)
            for qi in range(nqb):
                hop2 = pl.ds(qi * qb * dil + r0, qb * d_out, stride=d_in)
                o_ref[0, hop2, :] = o_tmp[qi]
                put_lse(hop2, l_tmp[qi])
            return carry

        lax.fori_loop(0, d_in, group, 0)


def _dilated_attention(qk, v, *, B, dil, half, heads):
    L = qk.shape[2]
    S = L * dil
    qb = min(DL_QB, L - 2 * half)
    assert L % qb == 0 and heads <= LANES
    d_out = dil // min(dil, DL_MAX_STRIDE)
    assert dil % min(dil, DL_MAX_STRIDE) == 0
    scratch = [pltpu.VMEM((3, qb, qb + 2 * half), F32)]
    if d_out > 1:
        scratch += [pltpu.VMEM((L // qb, qb * d_out, LANES), F32)] * 2
    blk = lambda off: pl.BlockSpec((1, dil, L, HEAD_DIM), lambda b, h: (b, 0, 0, h + off))
    return pl.pallas_call(
        functools.partial(_dilated_kernel, dil=dil, half=half, L=L),
        grid=(B, heads),
        in_specs=[blk(0), blk(heads), blk(0)],
        out_specs=[pl.BlockSpec((1, S, HEAD_DIM), lambda b, h: (b, 0, h)),
                   pl.BlockSpec((1, S, LANES), lambda b, h: (b, 0, 0))],
        out_shape=[jax.ShapeDtypeStruct((B, S, heads * HEAD_DIM), F32),
                   jax.ShapeDtypeStruct((B, S, LANES), F32)],
        scratch_shapes=scratch,
        compiler_params=_params("arbitrary", "arbitrary"),
        name=f"dilated_attention_d{dil}",
    )(qk, qk, v)


def _mix_out_kernel(*refs, n_groups, heads):
    o_refs = refs[:n_groups]
    l_refs = refs[n_groups:2 * n_groups]
    g_ref, w_ref, x_ref, nw_ref, out_ref, ya_ref, yb_ref = refs[2 * n_groups:]
    s = pl.program_id(0)
    D = out_ref.shape[1]

    @pl.when(s == 0)
    def _():
        yb_ref[...] = jnp.zeros_like(yb_ref)

    def step(y_fill, y_use):
        lses = [r[...] for r in l_refs]
        mx = functools.reduce(jnp.maximum, lses)
        es = [jnp.exp(l - mx) for l in lses]
        inv = 1.0 / functools.reduce(lambda a, b: a + b, es)
        wts = [e * inv for e in es]
        for h in range(heads):
            sl = slice(h * HEAD_DIM, (h + 1) * HEAD_DIM)
            o = None
            for wt, o_ref in zip(wts, o_refs):
                t = wt[:, h:h + 1] * o_ref[:, sl]
                o = t if o is None else o + t
            g = g_ref[:, sl]
            y_fill[:, sl] = (o * (g / (1.0 + jnp.exp(-g)))).astype(y_fill.dtype)

        ssq = None
        for c in range(D // MXU_COLS):
            cols = slice(c * MXU_COLS, (c + 1) * MXU_COLS)
            x = x_ref[:, cols] + jnp.dot(y_use[...], w_ref[:, cols], preferred_element_type=F32)
            out_ref[:, cols] = x
            sq = x * x
            for hh in range(MXU_COLS // LANES):
                part = sq[:, hh * LANES:(hh + 1) * LANES]
                ssq = part if ssq is None else ssq + part
        scale = lax.rsqrt(jnp.sum(ssq, axis=-1, keepdims=True) * (1.0 / D) + RMS_EPS)
        out_ref[...] = out_ref[...] * scale * nw_ref[...]

    @pl.when(s % 2 == 0)
    def _():
        step(ya_ref, yb_ref)

    @pl.when(s % 2 == 1)
    def _():
        step(yb_ref, ya_ref)


def _mix_out(os_, lses, gate, w, x, norm_w, *, heads, tm):
    M, D = x.shape
    width = heads * HEAD_DIM
    n = len(os_)
    n_tiles = M // tm
    fill = lambda c: pl.BlockSpec((tm, c), lambda s: (jnp.minimum(s, n_tiles - 1), 0))
    use = lambda c: pl.BlockSpec((tm, c), lambda s: (jnp.maximum(s - 1, 0), 0))
    return pl.pallas_call(
        functools.partial(_mix_out_kernel, n_groups=n, heads=heads),
        grid=(n_tiles + 1,),
        in_specs=[fill(width)] * n + [fill(LANES)] * n + [
            fill(width),
            pl.BlockSpec((width, D), lambda s: (0, 0), pipeline_mode=pl.Buffered(1)),
            use(D),
            pl.BlockSpec((1, D), lambda s: (0, 0))],
        out_specs=use(D),
        out_shape=jax.ShapeDtypeStruct((M, D), F32),
        scratch_shapes=[pltpu.VMEM((tm, width), BF16), pltpu.VMEM((tm, width), BF16)],
        compiler_params=_params("arbitrary"),
        name="mix_out_norm",
    )(*os_, *lses, gate, w, x, norm_w.reshape(1, D))


def _rope_tables(S):
    half = ROPE_DIM // 2
    inv_freq = jnp.power(ROPE_THETA, -jnp.arange(half, dtype=F32) * (2.0 / ROPE_DIM))
    ang = jnp.arange(S).astype(F32)[:, None] * inv_freq[None, :]
    cos, sin = jnp.cos(ang), jnp.sin(ang)
    zeros = jnp.zeros((S, LANES - ROPE_DIM), F32)
    z16 = jnp.zeros((S, half), F32)
    return (jnp.concatenate([cos, cos, 1.0 + zeros], axis=1),
            jnp.concatenate([z16, sin, zeros], axis=1),
            jnp.concatenate([-sin, z16, zeros], axis=1))


def _deinterleave(t, dil):
    S, C = t.shape
    return t.reshape(S // dil, dil, C).transpose(1, 0, 2).reshape(S, C)


def _forward(x, norm_w, final_norm_w, na_w_in, na_rpb, na_w_out, dl_w_in, dl_w_out, *,
             tm_norm=256, tm=1024, tm_plain=1024, tn_qkv0=768, tn=512, tm_out=256):
    B, S, D = x.shape
    M = B * S
    assert norm_w.shape[0] == 2 and na_w_in.shape[0] == 1 and dl_w_in.shape[0] == 1
    na_width = na_w_out.shape[1]
    na_heads = na_width // HEAD_DIM
    dl_width = dl_w_out.shape[1]
    dl_heads = dl_width // HEAD_DIM
    n_groups = len(DL_GROUPS)

    (xn0,) = _rmsnorm(x, norm_w[0], (1,), tm_norm)
    xn0 = xn0.reshape(M, D)
    (qkv0,) = _proj(xn0, na_w_in[0], [_Seg(0, 3 * na_width, tn_qkv0, BF16, n_q_cols=na_width)], tm=tm_plain,
                    name="proj0_qkv")
    (gate0,) = _proj(xn0, na_w_in[0], [_Seg(3 * na_width, na_width, tn, F32)], tm=tm_plain, name="proj0_gate")
    y0 = _na_attention(qkv0.reshape(B, S, 3 * na_width), gate0.reshape(B, S, na_width), na_rpb[0],
                       heads=na_heads)
    (x1,) = _proj(y0.reshape(M, na_width), na_w_out[0], [_Seg(0, D, tn, F32, mode="res")], tm=tm_plain,
                  res=x.reshape(M, D), name="out0")

    w_in1 = dl_w_in[0]
    dils = tuple(sorted({d for _, d in DL_GROUPS}))
    xn1 = dict(zip(dils, _rmsnorm(x1.reshape(B, S, D), norm_w[1], dils, tm_norm)))
    rope = _rope_tables(S)
    os_, lses = [], []
    for gi, (window, dil) in enumerate(DL_GROUPS):
        half = window // (2 * dil)
        off = gi * 3 * dl_width
        qk, v = _proj(xn1[dil].reshape(M, D), w_in1,
                      [_Seg(off, 2 * dl_width, tn, BF16, mode="rope", n_q_cols=dl_width),
                       _Seg(off + 2 * dl_width, dl_width, tn // 2, BF16)],
                      tm=tm, rope=[_deinterleave(t, dil) for t in rope], name=f"proj1_g{gi}")
        o, lse = _dilated_attention(qk.reshape(B, dil, S // dil, 2 * dl_width),
                                    v.reshape(B, dil, S // dil, dl_width), B=B, dil=dil, half=half,
                                    heads=dl_heads)
        os_.append(o.reshape(M, dl_width))
        lses.append(lse.reshape(M, LANES))
    (gate1,) = _proj(xn1[1].reshape(M, D), w_in1, [_Seg(3 * n_groups * dl_width, dl_width, tn, F32)], tm=tm_plain,
                     name="proj1_gate")
    out = _mix_out(os_, lses, gate1, dl_w_out[0].astype(BF16), x1, final_norm_w, heads=dl_heads, tm=tm_out)
    return out.reshape(B, S, D)


def kernel(x, norm_w, final_norm_w, na_w_in, na_rpb, na_w_out, dl_w_in, dl_w_out):
    return _forward(x, norm_w, final_norm_w, na_w_in, na_rpb, na_w_out, dl_w_in, dl_w_out)
```

```python
import functools
import math
from typing import Any, NamedTuple

import jax
import jax.numpy as jnp
from jax import lax
from jax.experimental import pallas as pl
from jax.experimental.pallas import tpu as pltpu

HEAD_DIM = 128
GRID_W = 64
NA_KH = 8
NA_KW = 16
DL_GROUPS = ((128, 1), (512, 4), (2048, 16))
ROPE_THETA = 500000.0
ROPE_DIM = HEAD_DIM // 4
RMS_EPS = 1e-6
NEG_INF = -1e30
SCALE = HEAD_DIM ** -0.5
LOG2E = math.log2(math.e)
LN2 = math.log(2.0)
Q_SCALE = SCALE * LOG2E

LANES = 128
MXU_COLS = 256
VMEM_LIMIT_BYTES = 56 * 1024 * 1024

BF16 = jnp.bfloat16
F32 = jnp.float32


def _params(*semantics):
    return pltpu.CompilerParams(dimension_semantics=semantics, vmem_limit_bytes=VMEM_LIMIT_BYTES)


def _attend(blocks):
    logits = [lax.dot_general(q, k, (((1,), (1,)), ((), ())), preferred_element_type=F32) + bias
              for q, k, _, bias in blocks]
    out = []
    for s, (_, _, v, _) in zip(logits, blocks):
        m = jnp.max(s, axis=-1, keepdims=True)
        p = jnp.exp2(s - m).astype(BF16)
        v_aug = jnp.concatenate([v, jnp.ones_like(v)], axis=1)
        pv = jnp.dot(p, v_aug, preferred_element_type=F32)
        den = pv[:, HEAD_DIM:]
        out.append((pv[:, :HEAD_DIM] / den, m, den))
    return out


def _rmsnorm_kernel(x_ref, g_ref, *rest, dils):
    out_refs, y_ref = rest[:-1], rest[-1]
    x = x_ref[0]
    y = x * lax.rsqrt(jnp.mean(x * x, axis=-1, keepdims=True) + RMS_EPS)
    y = y * g_ref[...]
    tm, D = x.shape
    n_chunks = D // LANES
    if any(d > 1 for d in dils):
        for c in range(n_chunks):
            y_ref[c] = y[:, c * LANES:(c + 1) * LANES]
    for d, o_ref in zip(dils, out_refs):
        if d == 1:
            o_ref[0, 0] = y.astype(o_ref.dtype)
        else:
            for c in range(n_chunks):
                for r in range(d):
                    o_ref[0, r, :, c * LANES:(c + 1) * LANES] = (
                        y_ref[c, pl.ds(r, tm // d, stride=d), :].astype(o_ref.dtype))


def _rmsnorm(x, g, dils, tm):
    B, S, D = x.shape
    out_shape = [jax.ShapeDtypeStruct((B, d, S // d, D), BF16) for d in dils]
    out_specs = [pl.BlockSpec((1, d, tm // d, D), lambda b, i: (b, 0, i, 0)) for d in dils]
    return pl.pallas_call(
        functools.partial(_rmsnorm_kernel, dils=dils),
        grid=(B, S // tm),
        in_specs=[pl.BlockSpec((1, tm, D), lambda b, i: (b, i, 0)),
                  pl.BlockSpec((1, D), lambda b, i: (0, 0))],
        out_specs=out_specs,
        out_shape=out_shape,
        scratch_shapes=[pltpu.VMEM((D // LANES, tm, LANES), F32)],
        compiler_params=_params("parallel", "parallel"),
        name="rmsnorm",
    )(x, g.reshape(1, D))


CAST_ROWS = 256


class _Seg(NamedTuple):
    col_off: int
    n_cols: int
    tn: int
    out_dtype: Any
    mode: str = "plain"
    n_q_cols: int = 0


def _proj_kernel(a_ref, *refs, segs, has_res, has_rope):
    n = len(segs)
    w_refs, refs = refs[:n], refs[n:]
    if has_res:
        res_ref, refs = refs[0], refs[1:]
    if has_rope:
        (cos_ref, shi_ref, slo_ref), refs = refs[:3], refs[3:]
    o_refs, wb_refs = refs[:n], refs[n:]
    j = pl.program_id(0)

    @pl.when(pl.program_id(1) == 0)
    def _convert_weights():
        for w_ref, wb_ref in zip(w_refs, wb_refs):
            def body(r, carry, w_ref=w_ref, wb_ref=wb_ref):
                rows = pl.ds(pl.multiple_of(r * CAST_ROWS, CAST_ROWS), CAST_ROWS)
                wb_ref[rows, :] = w_ref[rows, :].astype(BF16)
                return carry
            lax.fori_loop(0, w_ref.shape[0] // CAST_ROWS, body, 0)

    half = ROPE_DIM // 2
    for seg, wb_ref, o_ref in zip(segs, wb_refs, o_refs):
        for c in range(seg.tn // MXU_COLS):
            cols = slice(c * MXU_COLS, (c + 1) * MXU_COLS)
            hm = a_ref.shape[0] // 2
            acc = jnp.concatenate(
                [jnp.dot(a_ref[0:hm, :], wb_ref[:, cols], preferred_element_type=F32),
                 jnp.dot(a_ref[hm:, :], wb_ref[:, cols], preferred_element_type=F32)], axis=0)
            if seg.mode == "res":
                acc = res_ref[:, cols] + acc
            elif seg.mode == "rope":
                fac = jnp.where(j * seg.tn < seg.n_q_cols, Q_SCALE, 1.0).astype(F32)
                heads = []
                for hh in range(MXU_COLS // LANES):
                    x = acc[:, hh * LANES:(hh + 1) * LANES] * fac
                    heads.append(x * cos_ref[...] + pltpu.roll(x, half, 1) * shi_ref[...]
                                 + pltpu.roll(x, LANES - half, 1) * slo_ref[...])
                acc = jnp.concatenate(heads, axis=1)
            elif seg.n_q_cols:
                acc = acc * jnp.where(j * seg.tn + c * MXU_COLS < seg.n_q_cols, Q_SCALE, 1.0).astype(F32)
            o_ref[:, cols] = acc.astype(o_ref.dtype)


def _proj(a, w, segs, *, tm, res=None, rope=None, name):
    M, K = a.shape
    steps = segs[0].n_cols // segs[0].tn
    assert M % tm == 0 and K % CAST_ROWS == 0
    in_specs = [pl.BlockSpec((tm, K), lambda j, i: (i, 0))]
    args = [a]
    for seg in segs:
        assert seg.n_cols == steps * seg.tn and seg.col_off % seg.tn == 0 and seg.tn % MXU_COLS == 0
        in_specs.append(pl.BlockSpec((K, seg.tn), lambda j, i, off=seg.col_off // seg.tn: (0, j + off)))
        args.append(w)
    if res is not None:
        assert len(segs) == 1 and segs[0].mode == "res"
        in_specs.append(pl.BlockSpec((tm, segs[0].tn), lambda j, i: (i, j)))
        args.append(res)
    if rope is not None:
        (seg,) = [s for s in segs if s.mode == "rope"]
        P = rope[0].shape[0]
        assert P % tm == 0 and seg.n_q_cols % seg.tn == 0
        nper = P // tm
        for t in rope:
            in_specs.append(pl.BlockSpec((tm, LANES), lambda j, i: (i % nper, 0)))
            args.append(t)
    return pl.pallas_call(
        functools.partial(_proj_kernel, segs=tuple(segs), has_res=res is not None, has_rope=rope is not None),
        grid=(steps, M // tm),
        in_specs=in_specs,
        out_specs=[pl.BlockSpec((tm, seg.tn), lambda j, i: (i, j)) for seg in segs],
        out_shape=[jax.ShapeDtypeStruct((M, seg.n_cols), seg.out_dtype) for seg in segs],
        scratch_shapes=[pltpu.VMEM((K, seg.tn), BF16) for seg in segs],
        compiler_params=_params("arbitrary", "arbitrary"),
        name=name,
    )(*args)


NA_QROWS = 2
NA_KROWS = 10
NA_UNROLL = 16


def _na_block_tables(rows):
    nblk = rows // NA_QROWS
    kb0s, vids, variants = [], [], []
    for i in range(nblk):
        kb0 = min(max(NA_QROWS * i - NA_KH // 2, 0), rows - NA_KROWS)
        tab = []
        for a in range(NA_QROWS):
            r = NA_QROWS * i + a
            rs = min(max(r - NA_KH // 2, 0), rows - NA_KH)
            for kl in range(NA_KROWS):
                kr = kb0 + kl
                tab.append(kr - r + NA_KH - 1 if rs <= kr < rs + NA_KH else None)
        tab = tuple(tab)
        if tab not in variants:
            variants.append(tab)
        kb0s.append(kb0)
        vids.append(variants.index(tab))
    return kb0s, vids, variants


def _na_kernel(rpb_ref, kb0_ref, vid_ref, q_ref, k_ref, v_ref, g_ref, o_ref, tile_ref, bias_ref,
               *, variants, nblk):
    h = pl.program_id(0)
    W = GRID_W
    n_dr, n_dc = 2 * NA_KH - 1, 2 * NA_KW - 1

    @pl.when(pl.program_id(1) == 0)
    def _build_bias():
        qc = lax.broadcasted_iota(jnp.int32, (W, LANES), 0)
        lane = lax.broadcasted_iota(jnp.int32, (W, LANES), 1)
        kc = lane & (W - 1)
        cs = jnp.clip(qc - NA_KW // 2, 0, W - NA_KW)
        col_ok = (kc >= cs) & (kc < cs + NA_KW)
        dc = jnp.clip(kc - qc + NA_KW - 1, 0, n_dc - 1)

        def build_tile(dr, carry):
            t = jnp.zeros((W, LANES), F32)
            base = (h * n_dr + dr) * n_dc
            for j in range(n_dc):
                t = jnp.where(dc == j, rpb_ref[base + j] * LOG2E, t)
            tile_ref[dr] = jnp.where(col_ok, t, NEG_INF)
            return carry

        lax.fori_loop(0, n_dr, build_tile, 0)
        neg = jnp.full((W, LANES), NEG_INF, F32)
        for vi, tab in enumerate(variants):
            for a in range(NA_QROWS):
                for p in range(NA_KROWS // 2):
                    d0 = tab[a * NA_KROWS + 2 * p]
                    d1 = tab[a * NA_KROWS + 2 * p + 1]
                    t0 = neg if d0 is None else tile_ref[d0]
                    t1 = neg if d1 is None else tile_ref[d1]
                    bias_ref[vi, a * W:(a + 1) * W, p * LANES:(p + 1) * LANES] = jnp.where(lane < W, t0, t1)

    nq, nk = NA_QROWS * W, NA_KROWS * W

    def load(i):
        q0 = pl.multiple_of(i * nq, nq)
        k0 = pl.multiple_of(kb0_ref[i] * W, W)
        return (q_ref[0, pl.ds(q0, nq), :], k_ref[0, pl.ds(k0, nk), :], v_ref[0, pl.ds(k0, nk), :],
                bias_ref[vid_ref[i]])

    def group(n, carry):
        blocks = [n * NA_UNROLL + u for u in range(NA_UNROLL)]
        for i, (o, _, _) in zip(blocks, _attend([load(i) for i in blocks])):
            q0 = pl.multiple_of(i * nq, nq)
            g = g_ref[0, pl.ds(q0, nq), :]
            o_ref[0, pl.ds(q0, nq), :] = (o * (g / (1.0 + jnp.exp(-g)))).astype(o_ref.dtype)
        return carry

    lax.fori_loop(0, nblk // NA_UNROLL, group, 0)


def _na_attention(qkv, gate, rpb, *, heads):
    B, S, _ = qkv.shape
    rows = S // GRID_W
    assert rows % (NA_UNROLL * NA_QROWS) == 0 and rows >= NA_KROWS
    kb0s, vids, variants = _na_block_tables(rows)
    nblk = rows // NA_QROWS
    smem = pl.BlockSpec(memory_space=pltpu.SMEM)
    tok = lambda off: pl.BlockSpec((1, S, HEAD_DIM), lambda h, b: (b, 0, h + off))
    return pl.pallas_call(
        functools.partial(_na_kernel, variants=variants, nblk=nblk),
        grid=(heads, B),
        in_specs=[smem, smem, smem, tok(0), tok(heads), tok(2 * heads), tok(0)],
        out_specs=tok(0),
        out_shape=jax.ShapeDtypeStruct((B, S, heads * HEAD_DIM), BF16),
        scratch_shapes=[pltpu.VMEM((2 * NA_KH - 1, GRID_W, LANES), F32),
                        pltpu.VMEM((len(variants), NA_QROWS * GRID_W, NA_KROWS * GRID_W), F32)],
        compiler_params=_params("arbitrary", "arbitrary"),
        name="na_attention",
    )(rpb.reshape(-1), jnp.asarray(kb0s, jnp.int32), jnp.asarray(vids, jnp.int32), qkv, qkv, qkv, gate)


DL_QB = 128
DL_UNROLL = 16
DL_MAX_STRIDE = 4


def _dilated_kernel(q_ref, k_ref, v_ref, o_ref, lse_ref, mask_ref, *stage_refs, dil, half, L):
    h = pl.program_id(1)
    qb = min(DL_QB, L - 2 * half)
    kb = qb + 2 * half
    nqb = L // qb
    d_in = min(dil, DL_MAX_STRIDE)
    d_out = dil // d_in

    @pl.when(h == 0)
    def _():
        lse_ref[...] = jnp.zeros_like(lse_ref)

    rel = (lax.broadcasted_iota(jnp.int32, (qb, kb), 1) - lax.broadcasted_iota(jnp.int32, (qb, kb), 0))
    for vi in range(3):
        mask_ref[vi] = jnp.where(jnp.abs(rel - vi * half) <= half, 0.0, NEG_INF)

    def load(r, q0):
        k0 = pl.multiple_of(jnp.clip(q0 - half, 0, L - kb), half)
        return (q_ref[0, r, pl.ds(q0, qb), :], k_ref[0, r, pl.ds(k0, kb), :], v_ref[0, r, pl.ds(k0, kb), :],
                mask_ref[(q0 - k0) // half])

    def put_lse(rows, lse):
        lane = lax.broadcasted_iota(jnp.int32, lse.shape, 1)
        lse_ref[0, rows, :] = jnp.where(lane == h, lse, lse_ref[0, rows, :])

    if d_out == 1:
        unroll = math.gcd(dil * nqb, DL_UNROLL)

        def group(t, carry):
            blocks = [((t * unroll + u) // nqb, pl.multiple_of(((t * unroll + u) % nqb) * qb, qb))
                      for u in range(unroll)]
            for (r, q0), (o, m, den) in zip(blocks, _attend([load(r, q0) for r, q0 in blocks])):
                rows = pl.ds(q0, qb) if dil == 1 else pl.ds(q0 * dil + r, qb, stride=dil)
                o_ref[0, rows, :] = o
                put_lse(rows, m * LN2 + jnp.log(den))
            return carry

        lax.fori_loop(0, dil * nqb // unroll, group, 0)
    else:
        o_tmp, l_tmp = stage_refs

        def group(r0, carry):
            blocks = [(qi, r1) for qi in range(nqb) for r1 in range(d_out)]
            outs = _attend([load(r1 * d_in + r0, qi * qb) for qi, r1 in blocks])
            for (qi, r1), (o, m, den) in zip(blocks, outs):
                hop1 = pl.ds(r1, qb, stride=d_out)
                o_tmp[qi, hop1, :] = o
                l_tmp[qi, hop1, :] = m * LN2 + jnp.log(den)
            for qi in range(nqb):
                hop2 = pl.ds(qi * qb * dil + r0, qb * d_out, stride=d_in)
                o_ref[0, hop2, :] = o_tmp[qi]
                put_lse(hop2, l_tmp[qi])
            return carry

        lax.fori_loop(0, d_in, group, 0)


def _dilated_attention(qk, v, *, B, dil, half, heads):
    L = qk.shape[2]
    S = L * dil
    qb = min(DL_QB, L - 2 * half)
    assert L % qb == 0 and heads <= LANES
    d_out = dil // min(dil, DL_MAX_STRIDE)
    assert dil % min(dil, DL_MAX_STRIDE) == 0
    scratch = [pltpu.VMEM((3, qb, qb + 2 * half), F32)]
    if d_out > 1:
        scratch += [pltpu.VMEM((L // qb, qb * d_out, LANES), F32)] * 2
    blk = lambda off: pl.BlockSpec((1, dil, L, HEAD_DIM), lambda b, h: (b, 0, 0, h + off))
    return pl.pallas_call(
        functools.partial(_dilated_kernel, dil=dil, half=half, L=L),
        grid=(B, heads),
        in_specs=[blk(0), blk(heads), blk(0)],
        out_specs=[pl.BlockSpec((1, S, HEAD_DIM), lambda b, h: (b, 0, h)),
                   pl.BlockSpec((1, S, LANES), lambda b, h: (b, 0, 0))],
        out_shape=[jax.ShapeDtypeStruct((B, S, heads * HEAD_DIM), F32),
                   jax.ShapeDtypeStruct((B, S, LANES), F32)],
        scratch_shapes=scratch,
        compiler_params=_params("arbitrary", "arbitrary"),
        name=f"dilated_attention_d{dil}",
    )(qk, qk, v)


def _mix_out_kernel(*refs, n_groups, heads):
    o_refs = refs[:n_groups]
    l_refs = refs[n_groups:2 * n_groups]
    g_ref, w_ref, x_ref, nw_ref, out_ref, ya_ref, yb_ref = refs[2 * n_groups:]
    s = pl.program_id(0)
    D = out_ref.shape[1]

    @pl.when(s == 0)
    def _():
        yb_ref[...] = jnp.zeros_like(yb_ref)

    def step(y_fill, y_use):
        lses = [r[...] for r in l_refs]
        mx = functools.reduce(jnp.maximum, lses)
        es = [jnp.exp(l - mx) for l in lses]
        inv = 1.0 / functools.reduce(lambda a, b: a + b, es)
        wts = [e * inv for e in es]
        for h in range(heads):
            sl = slice(h * HEAD_DIM, (h + 1) * HEAD_DIM)
            o = None
            for wt, o_ref in zip(wts, o_refs):
                t = wt[:, h:h + 1] * o_ref[:, sl]
                o = t if o is None else o + t
            g = g_ref[:, sl]
            y_fill[:, sl] = (o * (g / (1.0 + jnp.exp(-g)))).astype(y_fill.dtype)

        ssq = None
        for c in range(D // MXU_COLS):
            cols = slice(c * MXU_COLS, (c + 1) * MXU_COLS)
            x = x_ref[:, cols] + jnp.dot(y_use[...], w_ref[:, cols], preferred_element_type=F32)
            out_ref[:, cols] = x
            sq = x * x
            for hh in range(MXU_COLS // LANES):
                part = sq[:, hh * LANES:(hh + 1) * LANES]
                ssq = part if ssq is None else ssq + part
        scale = lax.rsqrt(jnp.sum(ssq, axis=-1, keepdims=True) * (1.0 / D) + RMS_EPS)
        out_ref[...] = out_ref[...] * scale * nw_ref[...]

    @pl.when(s % 2 == 0)
    def _():
        step(ya_ref, yb_ref)

    @pl.when(s % 2 == 1)
    def _():
        step(yb_ref, ya_ref)


def _mix_out(os_, lses, gate, w, x, norm_w, *, heads, tm):
    M, D = x.shape
    width = heads * HEAD_DIM
    n = len(os_)
    n_tiles = M // tm
    fill = lambda c: pl.BlockSpec((tm, c), lambda s: (jnp.minimum(s, n_tiles - 1), 0))
    use = lambda c: pl.BlockSpec((tm, c), lambda s: (jnp.maximum(s - 1, 0), 0))
    return pl.pallas_call(
        functools.partial(_mix_out_kernel, n_groups=n, heads=heads),
        grid=(n_tiles + 1,),
        in_specs=[fill(width)] * n + [fill(LANES)] * n + [
            fill(width),
            pl.BlockSpec((width, D), lambda s: (0, 0), pipeline_mode=pl.Buffered(1)),
            use(D),
            pl.BlockSpec((1, D), lambda s: (0, 0))],
        out_specs=use(D),
        out_shape=jax.ShapeDtypeStruct((M, D), F32),
        scratch_shapes=[pltpu.VMEM((tm, width), BF16), pltpu.VMEM((tm, width), BF16)],
        compiler_params=_params("arbitrary"),
        name="mix_out_norm",
    )(*os_, *lses, gate, w, x, norm_w.reshape(1, D))


def _rope_tables(S):
    half = ROPE_DIM // 2
    inv_freq = jnp.power(ROPE_THETA, -jnp.arange(half, dtype=F32) * (2.0 / ROPE_DIM))
    ang = jnp.arange(S).astype(F32)[:, None] * inv_freq[None, :]
    cos, sin = jnp.cos(ang), jnp.sin(ang)
    zeros = jnp.zeros((S, LANES - ROPE_DIM), F32)
    z16 = jnp.zeros((S, half), F32)
    return (jnp.concatenate([cos, cos, 1.0 + zeros], axis=1),
            jnp.concatenate([z16, sin, zeros], axis=1),
            jnp.concatenate([-sin, z16, zeros], axis=1))


def _deinterleave(t, dil):
    S, C = t.shape
    return t.reshape(S // dil, dil, C).transpose(1, 0, 2).reshape(S, C)


def _forward(x, norm_w, final_norm_w, na_w_in, na_rpb, na_w_out, dl_w_in, dl_w_out, *,
             tm_norm=512, tm_norm1=256, tm=1024, tm_plain=1024, tn_qkv0=768, tn=512, tm_out=256):
    B, S, D = x.shape
    M = B * S
    assert norm_w.shape[0] == 2 and na_w_in.shape[0] == 1 and dl_w_in.shape[0] == 1
    na_width = na_w_out.shape[1]
    na_heads = na_width // HEAD_DIM
    dl_width = dl_w_out.shape[1]
    dl_heads = dl_width // HEAD_DIM
    n_groups = len(DL_GROUPS)

    (xn0,) = _rmsnorm(x, norm_w[0], (1,), tm_norm)
    xn0 = xn0.reshape(M, D)
    (qkv0,) = _proj(xn0, na_w_in[0], [_Seg(0, 3 * na_width, tn_qkv0, BF16, n_q_cols=na_width)], tm=tm_plain,
                    name="proj0_qkv")
    (gate0,) = _proj(xn0, na_w_in[0], [_Seg(3 * na_width, na_width, tn, F32)], tm=tm_plain, name="proj0_gate")
    y0 = _na_attention(qkv0.reshape(B, S, 3 * na_width), gate0.reshape(B, S, na_width), na_rpb[0],
                       heads=na_heads)
    (x1,) = _proj(y0.reshape(M, na_width), na_w_out[0], [_Seg(0, D, tn, F32, mode="res")], tm=tm_plain,
                  res=x.reshape(M, D), name="out0")

    w_in1 = dl_w_in[0]
    dils = tuple(sorted({d for _, d in DL_GROUPS}))
    xn1 = dict(zip(dils, _rmsnorm(x1.reshape(B, S, D), norm_w[1], dils, tm_norm1)))
    rope = _rope_tables(S)
    os_, lses = [], []
    for gi, (window, dil) in enumerate(DL_GROUPS):
        half = window // (2 * dil)
        off = gi * 3 * dl_width
        qk, v = _proj(xn1[dil].reshape(M, D), w_in1,
                      [_Seg(off, 2 * dl_width, tn, BF16, mode="rope", n_q_cols=dl_width),
                       _Seg(off + 2 * dl_width, dl_width, tn // 2, BF16)],
                      tm=tm, rope=[_deinterleave(t, dil) for t in rope], name=f"proj1_g{gi}")
        o, lse = _dilated_attention(qk.reshape(B, dil, S // dil, 2 * dl_width),
                                    v.reshape(B, dil, S // dil, dl_width), B=B, dil=dil, half=half,
                                    heads=dl_heads)
        os_.append(o.reshape(M, dl_width))
        lses.append(lse.reshape(M, LANES))
    (gate1,) = _proj(xn1[1].reshape(M, D), w_in1, [_Seg(3 * n_groups * dl_width, dl_width, tn, F32)], tm=tm_plain,
                     name="proj1_gate")
    out = _mix_out(os_, lses, gate1, dl_w_out[0].astype(BF16), x1, final_norm_w, heads=dl_heads, tm=tm_out)
    return out.reshape(B, S, D)


def kernel(x, norm_w, final_norm_w, na_w_in, na_rpb, na_w_out, dl_w_in, dl_w_out):
    return _forward(x, norm_w, final_norm_w, na_w_in, na_rpb, na_w_out, dl_w_in, dl_w_out)
```

```python
import functools
import math
from typing import Any, NamedTuple

import jax
import jax.numpy as jnp
from jax import lax
from jax.experimental import pallas as pl
from jax.experimental.pallas import tpu as pltpu

HEAD_DIM = 128
GRID_W = 64
NA_KH = 8
NA_KW = 16
DL_GROUPS = ((128, 1), (512, 4), (2048, 16))
ROPE_THETA = 500000.0
ROPE_DIM = HEAD_DIM // 4
RMS_EPS = 1e-6
NEG_INF = -1e30
SCALE = HEAD_DIM ** -0.5
LOG2E = math.log2(math.e)
LN2 = math.log(2.0)
Q_SCALE = SCALE * LOG2E

LANES = 128
MXU_COLS = 256
VMEM_LIMIT_BYTES = 56 * 1024 * 1024

BF16 = jnp.bfloat16
F32 = jnp.float32


def _params(*semantics):
    return pltpu.CompilerParams(dimension_semantics=semantics, vmem_limit_bytes=VMEM_LIMIT_BYTES)


def _attend(blocks):
    logits = [lax.dot_general(q, k, (((1,), (1,)), ((), ())), preferred_element_type=F32) + bias
              for q, k, _, bias in blocks]
    out = []
    for s, (_, _, v, _) in zip(logits, blocks):
        m = jnp.max(s, axis=-1, keepdims=True)
        p = jnp.exp2(s - m).astype(BF16)
        v_aug = jnp.concatenate([v, jnp.ones_like(v)], axis=1)
        pv = jnp.dot(p, v_aug, preferred_element_type=F32)
        den = pv[:, HEAD_DIM:]
        out.append((pv[:, :HEAD_DIM] / den, m, den))
    return out


def _rmsnorm_kernel(x_ref, g_ref, *rest, dils):
    out_refs, y_ref = rest[:-1], rest[-1]
    x = x_ref[0]
    y = x * lax.rsqrt(jnp.mean(x * x, axis=-1, keepdims=True) + RMS_EPS)
    y = y * g_ref[...]
    tm, D = x.shape
    n_chunks = D // LANES
    if any(d > 1 for d in dils):
        for c in range(n_chunks):
            y_ref[c] = y[:, c * LANES:(c + 1) * LANES]
    for d, o_ref in zip(dils, out_refs):
        if d == 1:
            o_ref[0, 0] = y.astype(o_ref.dtype)
        else:
            for c in range(n_chunks):
                for r in range(d):
                    o_ref[0, r, :, c * LANES:(c + 1) * LANES] = (
                        y_ref[c, pl.ds(r, tm // d, stride=d), :].astype(o_ref.dtype))


def _rmsnorm(x, g, dils, tm):
    B, S, D = x.shape
    out_shape = [jax.ShapeDtypeStruct((B, d, S // d, D), BF16) for d in dils]
    out_specs = [pl.BlockSpec((1, d, tm // d, D), lambda b, i: (b, 0, i, 0)) for d in dils]
    return pl.pallas_call(
        functools.partial(_rmsnorm_kernel, dils=dils),
        grid=(B, S // tm),
        in_specs=[pl.BlockSpec((1, tm, D), lambda b, i: (b, i, 0)),
                  pl.BlockSpec((1, D), lambda b, i: (0, 0))],
        out_specs=out_specs,
        out_shape=out_shape,
        scratch_shapes=[pltpu.VMEM((D // LANES, tm, LANES), F32)],
        compiler_params=_params("parallel", "parallel"),
        name="rmsnorm",
    )(x, g.reshape(1, D))


CAST_ROWS = 256


class _Seg(NamedTuple):
    col_off: int
    n_cols: int
    tn: int
    out_dtype: Any
    mode: str = "plain"
    n_q_cols: int = 0
    rot: Any = None


def _proj_kernel(a_ref, *refs, segs, has_res, has_rope):
    n = len(segs)
    w_refs, refs = refs[:n], refs[n:]
    if has_res:
        res_ref, refs = refs[0], refs[1:]
    if has_rope:
        (cos_ref, shi_ref, slo_ref), refs = refs[:3], refs[3:]
    o_refs, wb_refs = refs[:n], refs[n:]
    j = pl.program_id(0)

    @pl.when(pl.program_id(1) == 0)
    def _convert_weights():
        for w_ref, wb_ref in zip(w_refs, wb_refs):
            def body(r, carry, w_ref=w_ref, wb_ref=wb_ref):
                rows = pl.ds(pl.multiple_of(r * CAST_ROWS, CAST_ROWS), CAST_ROWS)
                wb_ref[rows, :] = w_ref[rows, :].astype(BF16)
                return carry
            lax.fori_loop(0, w_ref.shape[0] // CAST_ROWS, body, 0)

    half = ROPE_DIM // 2
    for seg, wb_ref, o_ref in zip(segs, wb_refs, o_refs):
        for c in range(seg.tn // MXU_COLS):
            cols = slice(c * MXU_COLS, (c + 1) * MXU_COLS)
            hm = a_ref.shape[0] // 2
            acc = jnp.concatenate(
                [jnp.dot(a_ref[0:hm, :], wb_ref[:, cols], preferred_element_type=F32),
                 jnp.dot(a_ref[hm:, :], wb_ref[:, cols], preferred_element_type=F32)], axis=0)
            if seg.mode == "res":
                acc = res_ref[:, cols] + acc
            elif seg.mode == "rope":
                fac = jnp.where(j * seg.tn < seg.n_q_cols, Q_SCALE, 1.0).astype(F32)
                heads = []
                for hh in range(MXU_COLS // LANES):
                    x = acc[:, hh * LANES:(hh + 1) * LANES] * fac
                    heads.append(x * cos_ref[...] + pltpu.roll(x, half, 1) * shi_ref[...]
                                 + pltpu.roll(x, LANES - half, 1) * slo_ref[...])
                acc = jnp.concatenate(heads, axis=1)
            elif seg.n_q_cols:
                is_q = j * seg.tn + c * MXU_COLS < seg.n_q_cols
                if seg.rot:
                    acc = jnp.where(is_q, acc * Q_SCALE, _rot_rows(acc, *seg.rot))
                else:
                    acc = acc * jnp.where(is_q, Q_SCALE, 1.0).astype(F32)
            o_ref[:, cols] = acc.astype(o_ref.dtype)


def _proj(a, w, segs, *, tm, res=None, rope=None, name):
    M, K = a.shape
    steps = segs[0].n_cols // segs[0].tn
    assert M % tm == 0 and K % CAST_ROWS == 0
    in_specs = [pl.BlockSpec((tm, K), lambda j, i: (i, 0))]
    args = [a]
    for seg in segs:
        assert seg.n_cols == steps * seg.tn and seg.col_off % seg.tn == 0 and seg.tn % MXU_COLS == 0
        in_specs.append(pl.BlockSpec((K, seg.tn), lambda j, i, off=seg.col_off // seg.tn: (0, j + off)))
        args.append(w)
    if res is not None:
        assert len(segs) == 1 and segs[0].mode == "res"
        in_specs.append(pl.BlockSpec((tm, segs[0].tn), lambda j, i: (i, j)))
        args.append(res)
    if rope is not None:
        (seg,) = [s for s in segs if s.mode == "rope"]
        P = rope[0].shape[0]
        assert P % tm == 0 and seg.n_q_cols % seg.tn == 0
        nper = P // tm
        for t in rope:
            in_specs.append(pl.BlockSpec((tm, LANES), lambda j, i: (i % nper, 0)))
            args.append(t)
    return pl.pallas_call(
        functools.partial(_proj_kernel, segs=tuple(segs), has_res=res is not None, has_rope=rope is not None),
        grid=(steps, M // tm),
        in_specs=in_specs,
        out_specs=[pl.BlockSpec((tm, seg.tn), lambda j, i: (i, j)) for seg in segs],
        out_shape=[jax.ShapeDtypeStruct((M, seg.n_cols), seg.out_dtype) for seg in segs],
        scratch_shapes=[pltpu.VMEM((K, seg.tn), BF16) for seg in segs],
        compiler_params=_params("arbitrary", "arbitrary"),
        name=name,
    )(*args)


NA_QROWS = 8
NA_KROWS = 16
NA_QCOLS = 16
NA_KCOLS = 32
NA_ROT = NA_KW // 2
NA_RB = 4


def _rot_rows(x, group, shift):
    pieces = []
    for g in range(x.shape[0] // group):
        blk = x[g * group:(g + 1) * group]
        pieces += [blk[group - shift:], blk[:group - shift]]
    return jnp.concatenate(pieces, axis=0)


def _na_block_tables(rows):
    nblk = rows // NA_QROWS
    kb0s, vids, variants = [], [], []
    for i in range(nblk):
        kb0 = min(max(NA_QROWS * i - NA_KH // 2, 0), rows - NA_KROWS)
        tab = []
        for a in range(NA_QROWS):
            r = NA_QROWS * i + a
            rs = min(max(r - NA_KH // 2, 0), rows - NA_KH)
            for kl in range(NA_KROWS):
                kr = kb0 + kl
                tab.append(kr - r + NA_KH - 1 if rs <= kr < rs + NA_KH else None)
        tab = tuple(tab)
        if tab not in variants:
            variants.append(tab)
        kb0s.append(kb0)
        vids.append(variants.index(tab))
    return kb0s, vids, variants


def _na_kernel(rpb_ref, kb0_ref, vid_ref, q_ref, k_ref, v_ref, g_ref, o_ref, tile_ref, bias_ref,
               *, variants, nblk):
    h = pl.program_id(0)
    W = GRID_W
    n_ct = W // NA_QCOLS
    n_dr, n_dc = 2 * NA_KH - 1, 2 * NA_KW - 1
    per = LANES // NA_KCOLS

    @pl.when(pl.program_id(1) == 0)
    def _build_bias():
        qcl = lax.broadcasted_iota(jnp.int32, (NA_QCOLS, LANES), 0)
        lane = lax.broadcasted_iota(jnp.int32, (NA_QCOLS, LANES), 1)
        kcl = lane & (NA_KCOLS - 1)
        for ct in range(n_ct):
            qc = ct * NA_QCOLS + qcl
            kc = (ct * NA_QCOLS + kcl - NA_ROT) & (W - 1)
            cs = jnp.clip(qc - NA_KW // 2, 0, W - NA_KW)
            col_ok = (kc >= cs) & (kc < cs + NA_KW)
            dc = jnp.clip(kc - qc + NA_KW - 1, 0, n_dc - 1)

            def build_tile(dr, carry, ct=ct, dc=dc, col_ok=col_ok):
                t = jnp.zeros((NA_QCOLS, LANES), F32)
                base = (h * n_dr + dr) * n_dc
                for j in range(n_dc):
                    t = jnp.where(dc == j, rpb_ref[base + j] * LOG2E, t)
                tile_ref[ct, dr] = jnp.where(col_ok, t, NEG_INF)
                return carry

            lax.fori_loop(0, n_dr, build_tile, 0)
        neg = jnp.full((NA_QCOLS, LANES), NEG_INF, F32)
        part = lane // NA_KCOLS
        for vi, tab in enumerate(variants):
            for ct in range(n_ct):
                for a in range(NA_QROWS):
                    for m in range(NA_KROWS // per):
                        piece = neg
                        for i in range(per):
                            d = tab[a * NA_KROWS + per * m + i]
                            if d is not None:
                                piece = jnp.where(part == i, tile_ref[ct, d], piece)
                        bias_ref[vi, ct, a * NA_QCOLS:(a + 1) * NA_QCOLS, m * LANES:(m + 1) * LANES] = piece

    def rows_of(ref, row0, n_rows, cols):
        return jnp.concatenate(
            [ref[0, pl.ds(pl.multiple_of((row0 + a) * W + c, NA_QCOLS), NA_QCOLS), :]
             for a in range(n_rows) for c in cols], axis=0)

    def load(i, ct):
        kpos = [(ct * NA_QCOLS + u * NA_QCOLS) % W for u in range(NA_KCOLS // NA_QCOLS)]
        return (rows_of(q_ref, i * NA_QROWS, NA_QROWS, [ct * NA_QCOLS]),
                rows_of(k_ref, kb0_ref[i], NA_KROWS, kpos),
                rows_of(v_ref, kb0_ref[i], NA_KROWS, kpos),
                bias_ref[vid_ref[i], ct])

    def group(n, carry):
        tiles = [(n * NA_RB + rb, ct) for rb in range(NA_RB) for ct in range(n_ct)]
        for (i, ct), (o, _, _) in zip(tiles, _attend([load(i, ct) for i, ct in tiles])):
            g = rows_of(g_ref, i * NA_QROWS, NA_QROWS, [ct * NA_QCOLS])
            y = (o * (g / (1.0 + jnp.exp(-g)))).astype(o_ref.dtype)
            for a in range(NA_QROWS):
                start = pl.multiple_of((i * NA_QROWS + a) * W + ct * NA_QCOLS, NA_QCOLS)
                o_ref[0, pl.ds(start, NA_QCOLS), :] = y[a * NA_QCOLS:(a + 1) * NA_QCOLS, :]
        return carry

    lax.fori_loop(0, nblk // NA_RB, group, 0)


def _na_attention(qkv, gate, rpb, *, heads):
    B, S, _ = qkv.shape
    rows = S // GRID_W
    assert rows % (NA_RB * NA_QROWS) == 0 and rows >= NA_KROWS
    kb0s, vids, variants = _na_block_tables(rows)
    nblk = rows // NA_QROWS
    n_ct = GRID_W // NA_QCOLS
    smem = pl.BlockSpec(memory_space=pltpu.SMEM)
    tok = lambda off: pl.BlockSpec((1, S, HEAD_DIM), lambda h, b: (b, 0, h + off))
    return pl.pallas_call(
        functools.partial(_na_kernel, variants=variants, nblk=nblk),
        grid=(heads, B),
        in_specs=[smem, smem, smem, tok(0), tok(heads), tok(2 * heads), tok(0)],
        out_specs=tok(0),
        out_shape=jax.ShapeDtypeStruct((B, S, heads * HEAD_DIM), BF16),
        scratch_shapes=[pltpu.VMEM((n_ct, 2 * NA_KH - 1, NA_QCOLS, LANES), F32),
                        pltpu.VMEM((len(variants), n_ct, NA_QROWS * NA_QCOLS, NA_KROWS * NA_KCOLS), F32)],
        compiler_params=_params("arbitrary", "arbitrary"),
        name="na_attention",
    )(rpb.reshape(-1), jnp.asarray(kb0s, jnp.int32), jnp.asarray(vids, jnp.int32), qkv, qkv, qkv, gate)


DL_QB = 128
DL_UNROLL = 16
DL_MAX_STRIDE = 4


def _dilated_kernel(q_ref, k_ref, v_ref, o_ref, lse_ref, mask_ref, *stage_refs, dil, half, L):
    h = pl.program_id(1)
    qb = min(DL_QB, L - 2 * half)
    kb = qb + 2 * half
    nqb = L // qb
    d_in = min(dil, DL_MAX_STRIDE)
    d_out = dil // d_in

    @pl.when(h == 0)
    def _():
        lse_ref[...] = jnp.zeros_like(lse_ref)

    rel = (lax.broadcasted_iota(jnp.int32, (qb, kb), 1) - lax.broadcasted_iota(jnp.int32, (qb, kb), 0))
    for vi in range(3):
        mask_ref[vi] = jnp.where(jnp.abs(rel - vi * half) <= half, 0.0, NEG_INF)

    def load(r, q0):
        k0 = pl.multiple_of(jnp.clip(q0 - half, 0, L - kb), half)
        return (q_ref[0, r, pl.ds(q0, qb), :], k_ref[0, r, pl.ds(k0, kb), :], v_ref[0, r, pl.ds(k0, kb), :],
                mask_ref[(q0 - k0) // half])

    def put_lse(rows, lse):
        lane = lax.broadcasted_iota(jnp.int32, lse.shape, 1)
        lse_ref[0, rows, :] = jnp.where(lane == h, lse, lse_ref[0, rows, :])

    if d_out == 1:
        unroll = math.gcd(dil * nqb, DL_UNROLL)

        def group(t, carry):
            blocks = [((t * unroll + u) // nqb, pl.multiple_of(((t * unroll + u) % nqb) * qb, qb))
                      for u in range(unroll)]
            for (r, q0), (o, m, den) in zip(blocks, _attend([load(r, q0) for r, q0 in blocks])):
                rows = pl.ds(q0, qb) if dil == 1 else pl.ds(q0 * dil + r, qb, stride=dil)
                o_ref[0, rows, :] = o
                put_lse(rows, m * LN2 + jnp.log(den))
            return carry

        lax.fori_loop(0, dil * nqb // unroll, group, 0)
    else:
        o_tmp, l_tmp = stage_refs

        def group(r0, carry):
            blocks = [(qi, r1) for qi in range(nqb) for r1 in range(d_out)]
            outs = _attend([load(r1 * d_in + r0, qi * qb) for qi, r1 in blocks])
            for (qi, r1), (o, m, den) in zip(blocks, outs):
                hop1 = pl.ds(r1, qb, stride=d_out)
                o_tmp[qi, hop1, :] = o
                l_tmp[qi, hop1, :] = m * LN2 + jnp.log(den)
            for qi in range(nqb):
                hop2 = pl.ds(qi * qb * dil + r0, qb * d_out, stride=d_in)
                o_ref[0, hop2, :] = o_tmp[qi]
                put_lse(hop2, l_tmp[qi])
            return carry

        lax.fori_loop(0, d_in, group, 0)


def _dilated_attention(qk, v, *, B, dil, half, heads):
    L = qk.shape[2]
    S = L * dil
    qb = min(DL_QB, L - 2 * half)
    assert L % qb == 0 and heads <= LANES
    d_out = dil // min(dil, DL_MAX_STRIDE)
    assert dil % min(dil, DL_MAX_STRIDE) == 0
    scratch = [pltpu.VMEM((3, qb, qb + 2 * half), F32)]
    if d_out > 1:
        scratch += [pltpu.VMEM((L // qb, qb * d_out, LANES), F32)] * 2
    blk = lambda off: pl.BlockSpec((1, dil, L, HEAD_DIM), lambda b, h: (b, 0, 0, h + off))
    return pl.pallas_call(
        functools.partial(_dilated_kernel, dil=dil, half=half, L=L),
        grid=(B, heads),
        in_specs=[blk(0), blk(heads), blk(0)],
        out_specs=[pl.BlockSpec((1, S, HEAD_DIM), lambda b, h: (b, 0, h)),
                   pl.BlockSpec((1, S, LANES), lambda b, h: (b, 0, 0))],
        out_shape=[jax.ShapeDtypeStruct((B, S, heads * HEAD_DIM), F32),
                   jax.ShapeDtypeStruct((B, S, LANES), F32)],
        scratch_shapes=scratch,
        compiler_params=_params("arbitrary", "arbitrary"),
        name=f"dilated_attention_d{dil}",
    )(qk, qk, v)


def _mix_out_kernel(*refs, n_groups, heads):
    o_refs = refs[:n_groups]
    l_refs = refs[n_groups:2 * n_groups]
    g_ref, w_ref, x_ref, nw_ref, out_ref, ya_ref, yb_ref = refs[2 * n_groups:]
    s = pl.program_id(0)
    D = out_ref.shape[1]

    @pl.when(s == 0)
    def _():
        yb_ref[...] = jnp.zeros_like(yb_ref)

    def step(y_fill, y_use):
        lses = [r[...] for r in l_refs]
        mx = functools.reduce(jnp.maximum, lses)
        es = [jnp.exp(l - mx) for l in lses]
        inv = 1.0 / functools.reduce(lambda a, b: a + b, es)
        wts = [e * inv for e in es]
        for h in range(heads):
            sl = slice(h * HEAD_DIM, (h + 1) * HEAD_DIM)
            o = None
            for wt, o_ref in zip(wts, o_refs):
                t = wt[:, h:h + 1] * o_ref[:, sl]
                o = t if o is None else o + t
            g = g_ref[:, sl]
            y_fill[:, sl] = (o * (g / (1.0 + jnp.exp(-g)))).astype(y_fill.dtype)

        ssq = None
        for c in range(D // MXU_COLS):
            cols = slice(c * MXU_COLS, (c + 1) * MXU_COLS)
            x = x_ref[:, cols] + jnp.dot(y_use[...], w_ref[:, cols], preferred_element_type=F32)
            out_ref[:, cols] = x
            sq = x * x
            for hh in range(MXU_COLS // LANES):
                part = sq[:, hh * LANES:(hh + 1) * LANES]
                ssq = part if ssq is None else ssq + part
        scale = lax.rsqrt(jnp.sum(ssq, axis=-1, keepdims=True) * (1.0 / D) + RMS_EPS)
        out_ref[...] = out_ref[...] * scale * nw_ref[...]

    @pl.when(s % 2 == 0)
    def _():
        step(ya_ref, yb_ref)

    @pl.when(s % 2 == 1)
    def _():
        step(yb_ref, ya_ref)


def _mix_out(os_, lses, gate, w, x, norm_w, *, heads, tm):
    M, D = x.shape
    width = heads * HEAD_DIM
    n = len(os_)
    n_tiles = M // tm
    fill = lambda c: pl.BlockSpec((tm, c), lambda s: (jnp.minimum(s, n_tiles - 1), 0))
    use = lambda c: pl.BlockSpec((tm, c), lambda s: (jnp.maximum(s - 1, 0), 0))
    return pl.pallas_call(
        functools.partial(_mix_out_kernel, n_groups=n, heads=heads),
        grid=(n_tiles + 1,),
        in_specs=[fill(width)] * n + [fill(LANES)] * n + [
            fill(width),
            pl.BlockSpec((width, D), lambda s: (0, 0), pipeline_mode=pl.Buffered(1)),
            use(D),
            pl.BlockSpec((1, D), lambda s: (0, 0))],
        out_specs=use(D),
        out_shape=jax.ShapeDtypeStruct((M, D), F32),
        scratch_shapes=[pltpu.VMEM((tm, width), BF16), pltpu.VMEM((tm, width), BF16)],
        compiler_params=_params("arbitrary"),
        name="mix_out_norm",
    )(*os_, *lses, gate, w, x, norm_w.reshape(1, D))


def _rope_tables(S):
    half = ROPE_DIM // 2
    inv_freq = jnp.power(ROPE_THETA, -jnp.arange(half, dtype=F32) * (2.0 / ROPE_DIM))
    ang = jnp.arange(S).astype(F32)[:, None] * inv_freq[None, :]
    cos, sin = jnp.cos(ang), jnp.sin(ang)
    zeros = jnp.zeros((S, LANES - ROPE_DIM), F32)
    z16 = jnp.zeros((S, half), F32)
    return (jnp.concatenate([cos, cos, 1.0 + zeros], axis=1),
            jnp.concatenate([z16, sin, zeros], axis=1),
            jnp.concatenate([-sin, z16, zeros], axis=1))


def _deinterleave(t, dil):
    S, C = t.shape
    return t.reshape(S // dil, dil, C).transpose(1, 0, 2).reshape(S, C)


def _forward(x, norm_w, final_norm_w, na_w_in, na_rpb, na_w_out, dl_w_in, dl_w_out, *,
             tm_norm=512, tm_norm1=256, tm=1024, tm_plain=1024, tn_qkv0=768, tn=512, tm_out=256):
    B, S, D = x.shape
    M = B * S
    assert norm_w.shape[0] == 2 and na_w_in.shape[0] == 1 and dl_w_in.shape[0] == 1
    na_width = na_w_out.shape[1]
    na_heads = na_width // HEAD_DIM
    dl_width = dl_w_out.shape[1]
    dl_heads = dl_width // HEAD_DIM
    n_groups = len(DL_GROUPS)

    (xn0,) = _rmsnorm(x, norm_w[0], (1,), tm_norm)
    xn0 = xn0.reshape(M, D)
    (qkv0,) = _proj(xn0, na_w_in[0], [_Seg(0, 3 * na_width, tn_qkv0, BF16, n_q_cols=na_width, rot=(GRID_W, NA_ROT))],
                    tm=tm_plain,
                    name="proj0_qkv")
    (gate0,) = _proj(xn0, na_w_in[0], [_Seg(3 * na_width, na_width, tn, F32)], tm=tm_plain, name="proj0_gate")
    y0 = _na_attention(qkv0.reshape(B, S, 3 * na_width), gate0.reshape(B, S, na_width), na_rpb[0],
                       heads=na_heads)
    (x1,) = _proj(y0.reshape(M, na_width), na_w_out[0], [_Seg(0, D, tn, F32, mode="res")], tm=tm_plain,
                  res=x.reshape(M, D), name="out0")

    w_in1 = dl_w_in[0]
    dils = tuple(sorted({d for _, d in DL_GROUPS}))
    xn1 = dict(zip(dils, _rmsnorm(x1.reshape(B, S, D), norm_w[1], dils, tm_norm1)))
    rope = _rope_tables(S)
    os_, lses = [], []
    for gi, (window, dil) in enumerate(DL_GROUPS):
        half = window // (2 * dil)
        off = gi * 3 * dl_width
        qk, v = _proj(xn1[dil].reshape(M, D), w_in1,
                      [_Seg(off, 2 * dl_width, tn, BF16, mode="rope", n_q_cols=dl_width),
                       _Seg(off + 2 * dl_width, dl_width, tn // 2, BF16)],
                      tm=tm, rope=[_deinterleave(t, dil) for t in rope], name=f"proj1_g{gi}")
        o, lse = _dilated_attention(qk.reshape(B, dil, S // dil, 2 * dl_width),
                                    v.reshape(B, dil, S // dil, dl_width), B=B, dil=dil, half=half,
                                    heads=dl_heads)
        os_.append(o.reshape(M, dl_width))
        lses.append(lse.reshape(M, LANES))
    (gate1,) = _proj(xn1[1].reshape(M, D), w_in1, [_Seg(3 * n_groups * dl_width, dl_width, tn, F32)], tm=tm_plain,
                     name="proj1_gate")
    out = _mix_out(os_, lses, gate1, dl_w_out[0].astype(BF16), x1, final_norm_w, heads=dl_heads, tm=tm_out)
    return out.reshape(B, S, D)


def kernel(x, norm_w, final_norm_w, na_w_in, na_rpb, na_w_out, dl_w_in, dl_w_out):
    return _forward(x, norm_w, final_norm_w, na_w_in, na_rpb, na_w_out, dl_w_in, dl_w_out)
```

```python
import functools
import math
from typing import Any, NamedTuple

import jax
import jax.numpy as jnp
from jax import lax
from jax.experimental import pallas as pl
from jax.experimental.pallas import tpu as pltpu

HEAD_DIM = 128
GRID_W = 64
NA_KH = 8
NA_KW = 16
DL_GROUPS = ((128, 1), (512, 4), (2048, 16))
ROPE_THETA = 500000.0
ROPE_DIM = HEAD_DIM // 4
RMS_EPS = 1e-6
NEG_INF = -1e30
SCALE = HEAD_DIM ** -0.5
LOG2E = math.log2(math.e)
LN2 = math.log(2.0)
Q_SCALE = SCALE * LOG2E

LANES = 128
MXU_COLS = 256
VMEM_LIMIT_BYTES = 56 * 1024 * 1024

BF16 = jnp.bfloat16
F32 = jnp.float32


def _params(*semantics):
    return pltpu.CompilerParams(dimension_semantics=semantics, vmem_limit_bytes=VMEM_LIMIT_BYTES)


def _silu(g):
    hg = 0.5 * g
    return hg + hg * jnp.tanh(hg)


def _attend(blocks):
    logits = [lax.dot_general(q, k, (((1,), (1,)), ((), ())), preferred_element_type=F32) + bias
              for q, k, _, bias in blocks]
    out = []
    for s, (_, _, v, _) in zip(logits, blocks):
        m = jnp.max(s, axis=-1, keepdims=True)
        p = jnp.exp2(s - m).astype(BF16)
        v_aug = jnp.concatenate([v, jnp.ones_like(v)], axis=1)
        pv = jnp.dot(p, v_aug, preferred_element_type=F32)
        den = pv[:, HEAD_DIM:]
        out.append((pv[:, :HEAD_DIM] / den, m, den))
    return out


def _rmsnorm_kernel(x_ref, g_ref, *rest, dils):
    out_refs, y_ref = rest[:-1], rest[-1]
    x = x_ref[0]
    y = x * lax.rsqrt(jnp.mean(x * x, axis=-1, keepdims=True) + RMS_EPS)
    y = y * g_ref[...]
    tm, D = x.shape
    n_chunks = D // LANES
    if any(d > 1 for d in dils):
        for c in range(n_chunks):
            y_ref[c] = y[:, c * LANES:(c + 1) * LANES]
    for d, o_ref in zip(dils, out_refs):
        if d == 1:
            o_ref[0, 0] = y.astype(o_ref.dtype)
        else:
            for c in range(n_chunks):
                for r in range(d):
                    o_ref[0, r, :, c * LANES:(c + 1) * LANES] = (
                        y_ref[c, pl.ds(r, tm // d, stride=d), :].astype(o_ref.dtype))


def _rmsnorm(x, g, dils, tm):
    B, S, D = x.shape
    out_shape = [jax.ShapeDtypeStruct((B, d, S // d, D), BF16) for d in dils]
    out_specs = [pl.BlockSpec((1, d, tm // d, D), lambda b, i: (b, 0, i, 0)) for d in dils]
    return pl.pallas_call(
        functools.partial(_rmsnorm_kernel, dils=dils),
        grid=(B, S // tm),
        in_specs=[pl.BlockSpec((1, tm, D), lambda b, i: (b, i, 0)),
                  pl.BlockSpec((1, D), lambda b, i: (0, 0))],
        out_specs=out_specs,
        out_shape=out_shape,
        scratch_shapes=[pltpu.VMEM((D // LANES, tm, LANES), F32)],
        compiler_params=_params("parallel", "parallel"),
        name="rmsnorm",
    )(x, g.reshape(1, D))


CAST_ROWS = 256


class _Seg(NamedTuple):
    col_off: int
    n_cols: int
    tn: int
    out_dtype: Any
    mode: str = "plain"
    n_q_cols: int = 0
    rot: Any = None


def _proj_kernel(a_ref, *refs, segs, has_res, has_rope):
    n = len(segs)
    w_refs, refs = refs[:n], refs[n:]
    if has_res:
        res_ref, refs = refs[0], refs[1:]
    if has_rope:
        (cos_ref, shi_ref, slo_ref), refs = refs[:3], refs[3:]
    o_refs, wb_refs = refs[:n], refs[n:]
    j = pl.program_id(0)

    @pl.when(pl.program_id(1) == 0)
    def _convert_weights():
        for w_ref, wb_ref in zip(w_refs, wb_refs):
            def body(r, carry, w_ref=w_ref, wb_ref=wb_ref):
                rows = pl.ds(pl.multiple_of(r * CAST_ROWS, CAST_ROWS), CAST_ROWS)
                wb_ref[rows, :] = w_ref[rows, :].astype(BF16)
                return carry
            lax.fori_loop(0, w_ref.shape[0] // CAST_ROWS, body, 0)

    half = ROPE_DIM // 2
    for seg, wb_ref, o_ref in zip(segs, wb_refs, o_refs):
        for c in range(seg.tn // MXU_COLS):
            cols = slice(c * MXU_COLS, (c + 1) * MXU_COLS)
            hm = a_ref.shape[0] // 2
            acc = jnp.concatenate(
                [jnp.dot(a_ref[0:hm, :], wb_ref[:, cols], preferred_element_type=F32),
                 jnp.dot(a_ref[hm:, :], wb_ref[:, cols], preferred_element_type=F32)], axis=0)
            if seg.mode == "res":
                acc = res_ref[:, cols] + acc
            elif seg.mode == "rope":
                fac = jnp.where(j * seg.tn < seg.n_q_cols, Q_SCALE, 1.0).astype(F32)
                heads = []
                for hh in range(MXU_COLS // LANES):
                    x = acc[:, hh * LANES:(hh + 1) * LANES] * fac
                    heads.append(x * cos_ref[...] + pltpu.roll(x, half, 1) * shi_ref[...]
                                 + pltpu.roll(x, LANES - half, 1) * slo_ref[...])
                acc = jnp.concatenate(heads, axis=1)
            elif seg.n_q_cols:
                is_q = j * seg.tn + c * MXU_COLS < seg.n_q_cols
                if seg.rot:
                    acc = jnp.where(is_q, acc * Q_SCALE, _rot_rows(acc, *seg.rot))
                else:
                    acc = acc * jnp.where(is_q, Q_SCALE, 1.0).astype(F32)
            o_ref[:, cols] = acc.astype(o_ref.dtype)


def _proj(a, w, segs, *, tm, res=None, rope=None, name):
    M, K = a.shape
    steps = segs[0].n_cols // segs[0].tn
    assert M % tm == 0 and K % CAST_ROWS == 0
    in_specs = [pl.BlockSpec((tm, K), lambda j, i: (i, 0))]
    args = [a]
    for seg in segs:
        assert seg.n_cols == steps * seg.tn and seg.col_off % seg.tn == 0 and seg.tn % MXU_COLS == 0
        in_specs.append(pl.BlockSpec((K, seg.tn), lambda j, i, off=seg.col_off // seg.tn: (0, j + off)))
        args.append(w)
    if res is not None:
        assert len(segs) == 1 and segs[0].mode == "res"
        in_specs.append(pl.BlockSpec((tm, segs[0].tn), lambda j, i: (i, j)))
        args.append(res)
    if rope is not None:
        (seg,) = [s for s in segs if s.mode == "rope"]
        P = rope[0].shape[0]
        assert P % tm == 0 and seg.n_q_cols % seg.tn == 0
        nper = P // tm
        for t in rope:
            in_specs.append(pl.BlockSpec((tm, LANES), lambda j, i: (i % nper, 0)))
            args.append(t)
    return pl.pallas_call(
        functools.partial(_proj_kernel, segs=tuple(segs), has_res=res is not None, has_rope=rope is not None),
        grid=(steps, M // tm),
        in_specs=in_specs,
        out_specs=[pl.BlockSpec((tm, seg.tn), lambda j, i: (i, j)) for seg in segs],
        out_shape=[jax.ShapeDtypeStruct((M, seg.n_cols), seg.out_dtype) for seg in segs],
        scratch_shapes=[pltpu.VMEM((K, seg.tn), BF16) for seg in segs],
        compiler_params=_params("arbitrary", "arbitrary"),
        name=name,
    )(*args)


NA_QROWS = 8
NA_KROWS = 16
NA_QCOLS = 16
NA_KCOLS = 32
NA_ROT = NA_KW // 2
NA_RB = 8


def _rot_rows(x, group, shift):
    pieces = []
    for g in range(x.shape[0] // group):
        blk = x[g * group:(g + 1) * group]
        pieces += [blk[group - shift:], blk[:group - shift]]
    return jnp.concatenate(pieces, axis=0)


def _na_block_tables(rows):
    nblk = rows // NA_QROWS
    kb0s, vids, variants = [], [], []
    for i in range(nblk):
        kb0 = min(max(NA_QROWS * i - NA_KH // 2, 0), rows - NA_KROWS)
        tab = []
        for a in range(NA_QROWS):
            r = NA_QROWS * i + a
            rs = min(max(r - NA_KH // 2, 0), rows - NA_KH)
            for kl in range(NA_KROWS):
                kr = kb0 + kl
                tab.append(kr - r + NA_KH - 1 if rs <= kr < rs + NA_KH else None)
        tab = tuple(tab)
        if tab not in variants:
            variants.append(tab)
        kb0s.append(kb0)
        vids.append(variants.index(tab))
    return kb0s, vids, variants


def _na_kernel(rpb_ref, kb0_ref, vid_ref, q_ref, k_ref, v_ref, g_ref, o_ref, tile_ref, bias_ref,
               *, variants, nblk):
    h = pl.program_id(0)
    W = GRID_W
    n_ct = W // NA_QCOLS
    n_dr, n_dc = 2 * NA_KH - 1, 2 * NA_KW - 1
    per = LANES // NA_KCOLS

    @pl.when(pl.program_id(1) == 0)
    def _build_bias():
        qcl = lax.broadcasted_iota(jnp.int32, (NA_QCOLS, LANES), 0)
        lane = lax.broadcasted_iota(jnp.int32, (NA_QCOLS, LANES), 1)
        kcl = lane & (NA_KCOLS - 1)
        for ct in range(n_ct):
            qc = ct * NA_QCOLS + qcl
            kc = (ct * NA_QCOLS + kcl - NA_ROT) & (W - 1)
            cs = jnp.clip(qc - NA_KW // 2, 0, W - NA_KW)
            col_ok = (kc >= cs) & (kc < cs + NA_KW)
            dc = jnp.clip(kc - qc + NA_KW - 1, 0, n_dc - 1)

            def build_tile(dr, carry, ct=ct, dc=dc, col_ok=col_ok):
                t = jnp.zeros((NA_QCOLS, LANES), F32)
                base = (h * n_dr + dr) * n_dc
                for j in range(n_dc):
                    t = jnp.where(dc == j, rpb_ref[base + j] * LOG2E, t)
                tile_ref[ct, dr] = jnp.where(col_ok, t, NEG_INF)
                return carry

            lax.fori_loop(0, n_dr, build_tile, 0)
        neg = jnp.full((NA_QCOLS, LANES), NEG_INF, F32)
        part = lane // NA_KCOLS
        for vi, tab in enumerate(variants):
            for ct in range(n_ct):
                for a in range(NA_QROWS):
                    for m in range(NA_KROWS // per):
                        piece = neg
                        for i in range(per):
                            d = tab[a * NA_KROWS + per * m + i]
                            if d is not None:
                                piece = jnp.where(part == i, tile_ref[ct, d], piece)
                        bias_ref[vi, ct, a * NA_QCOLS:(a + 1) * NA_QCOLS, m * LANES:(m + 1) * LANES] = piece

    def rows_of(ref, row0, n_rows, cols):
        band = ref.at[0, pl.ds(pl.multiple_of(row0 * W, W), n_rows * W), :]
        return jnp.concatenate(
            [band[a * W + c:a * W + c + NA_QCOLS, :] for a in range(n_rows) for c in cols], axis=0)

    def load(i, ct):
        kpos = [(ct * NA_QCOLS + u * NA_QCOLS) % W for u in range(NA_KCOLS // NA_QCOLS)]
        return (rows_of(q_ref, i * NA_QROWS, NA_QROWS, [ct * NA_QCOLS]),
                rows_of(k_ref, kb0_ref[i], NA_KROWS, kpos),
                rows_of(v_ref, kb0_ref[i], NA_KROWS, kpos),
                bias_ref[vid_ref[i], ct])

    def group(n, carry):
        tiles = [(n * NA_RB + rb, ct) for rb in range(NA_RB) for ct in range(n_ct)]
        for (i, ct), (o, _, _) in zip(tiles, _attend([load(i, ct) for i, ct in tiles])):
            g = rows_of(g_ref, i * NA_QROWS, NA_QROWS, [ct * NA_QCOLS])
            y = (o * _silu(g)).astype(o_ref.dtype)
            band = o_ref.at[0, pl.ds(pl.multiple_of(i * NA_QROWS * W, W), NA_QROWS * W), :]
            for a in range(NA_QROWS):
                band[a * W + ct * NA_QCOLS:a * W + (ct + 1) * NA_QCOLS, :] = y[a * NA_QCOLS:(a + 1) * NA_QCOLS, :]
        return carry

    lax.fori_loop(0, nblk // NA_RB, group, 0)


def _na_attention(qkv, gate, rpb, *, heads):
    B, S, _ = qkv.shape
    rows = S // GRID_W
    assert rows % (NA_RB * NA_QROWS) == 0 and rows >= NA_KROWS
    kb0s, vids, variants = _na_block_tables(rows)
    nblk = rows // NA_QROWS
    n_ct = GRID_W // NA_QCOLS
    smem = pl.BlockSpec(memory_space=pltpu.SMEM)
    tok = lambda off: pl.BlockSpec((1, S, HEAD_DIM), lambda h, b: (b, 0, h + off))
    return pl.pallas_call(
        functools.partial(_na_kernel, variants=variants, nblk=nblk),
        grid=(heads, B),
        in_specs=[smem, smem, smem, tok(0), tok(heads), tok(2 * heads), tok(0)],
        out_specs=tok(0),
        out_shape=jax.ShapeDtypeStruct((B, S, heads * HEAD_DIM), BF16),
        scratch_shapes=[pltpu.VMEM((n_ct, 2 * NA_KH - 1, NA_QCOLS, LANES), F32),
                        pltpu.VMEM((len(variants), n_ct, NA_QROWS * NA_QCOLS, NA_KROWS * NA_KCOLS), F32)],
        compiler_params=_params("arbitrary", "arbitrary"),
        name="na_attention",
    )(rpb.reshape(-1), jnp.asarray(kb0s, jnp.int32), jnp.asarray(vids, jnp.int32), qkv, qkv, qkv, gate)


DL_QB = 128
DL_UNROLL = 32
DL_MAX_STRIDE = 4


def _dilated_kernel(q_ref, k_ref, v_ref, o_ref, lse_ref, mask_ref, *stage_refs, dil, half, L):
    h = pl.program_id(1)
    qb = min(DL_QB, L - 2 * half)
    kb = qb + 2 * half
    nqb = L // qb
    d_in = min(dil, DL_MAX_STRIDE)
    d_out = dil // d_in

    @pl.when(h == 0)
    def _():
        lse_ref[...] = jnp.zeros_like(lse_ref)

    rel = (lax.broadcasted_iota(jnp.int32, (qb, kb), 1) - lax.broadcasted_iota(jnp.int32, (qb, kb), 0))
    for vi in range(3):
        mask_ref[vi] = jnp.where(jnp.abs(rel - vi * half) <= half, 0.0, NEG_INF)

    def load(r, q0):
        k0 = pl.multiple_of(jnp.clip(q0 - half, 0, L - kb), half)
        return (q_ref[0, r, pl.ds(q0, qb), :], k_ref[0, r, pl.ds(k0, kb), :], v_ref[0, r, pl.ds(k0, kb), :],
                mask_ref[(q0 - k0) // half])

    def put_lse(rows, lse):
        lane = lax.broadcasted_iota(jnp.int32, lse.shape, 1)
        lse_ref[0, rows, :] = jnp.where(lane == h, lse, lse_ref[0, rows, :])

    if d_out == 1:
        unroll = math.gcd(dil * nqb, DL_UNROLL)

        def group(t, carry):
            blocks = [((t * unroll + u) // nqb, pl.multiple_of(((t * unroll + u) % nqb) * qb, qb))
                      for u in range(unroll)]
            for (r, q0), (o, m, den) in zip(blocks, _attend([load(r, q0) for r, q0 in blocks])):
                rows = pl.ds(q0, qb) if dil == 1 else pl.ds(q0 * dil + r, qb, stride=dil)
                o_ref[0, rows, :] = o
                put_lse(rows, m * LN2 + jnp.log(den))
            return carry

        lax.fori_loop(0, dil * nqb // unroll, group, 0)
    else:
        o_tmp, l_tmp = stage_refs

        def group(r0, carry):
            blocks = [(qi, r1) for qi in range(nqb) for r1 in range(d_out)]
            outs = _attend([load(r1 * d_in + r0, qi * qb) for qi, r1 in blocks])
            for (qi, r1), (o, m, den) in zip(blocks, outs):
                hop1 = pl.ds(r1, qb, stride=d_out)
                o_tmp[qi, hop1, :] = o
                l_tmp[qi, hop1, :] = m * LN2 + jnp.log(den)
            for qi in range(nqb):
                hop2 = pl.ds(qi * qb * dil + r0, qb * d_out, stride=d_in)
                o_ref[0, hop2, :] = o_tmp[qi]
                put_lse(hop2, l_tmp[qi])
            return carry

        lax.fori_loop(0, d_in, group, 0)


def _dilated_attention(qk, v, *, B, dil, half, heads):
    L = qk.shape[2]
    S = L * dil
    qb = min(DL_QB, L - 2 * half)
    assert L % qb == 0 and heads <= LANES
    d_out = dil // min(dil, DL_MAX_STRIDE)
    assert dil % min(dil, DL_MAX_STRIDE) == 0
    scratch = [pltpu.VMEM((3, qb, qb + 2 * half), F32)]
    if d_out > 1:
        scratch += [pltpu.VMEM((L // qb, qb * d_out, LANES), F32)] * 2
    blk = lambda off: pl.BlockSpec((1, dil, L, HEAD_DIM), lambda b, h: (b, 0, 0, h + off))
    return pl.pallas_call(
        functools.partial(_dilated_kernel, dil=dil, half=half, L=L),
        grid=(B, heads),
        in_specs=[blk(0), blk(heads), blk(0)],
        out_specs=[pl.BlockSpec((1, S, HEAD_DIM), lambda b, h: (b, 0, h)),
                   pl.BlockSpec((1, S, LANES), lambda b, h: (b, 0, 0))],
        out_shape=[jax.ShapeDtypeStruct((B, S, heads * HEAD_DIM), F32),
                   jax.ShapeDtypeStruct((B, S, LANES), F32)],
        scratch_shapes=scratch,
        compiler_params=_params("arbitrary", "arbitrary"),
        name=f"dilated_attention_d{dil}",
    )(qk, qk, v)


def _mix_out_kernel(*refs, n_groups, heads):
    o_refs = refs[:n_groups]
    l_refs = refs[n_groups:2 * n_groups]
    g_ref, w_ref, x_ref, nw_ref, out_ref, ya_ref, yb_ref = refs[2 * n_groups:]
    s = pl.program_id(0)
    D = out_ref.shape[1]

    @pl.when(s == 0)
    def _():
        yb_ref[...] = jnp.zeros_like(yb_ref)

    def step(y_fill, y_use):
        lses = [r[...] for r in l_refs]
        mx = functools.reduce(jnp.maximum, lses)
        es = [jnp.exp(l - mx) for l in lses]
        inv = 1.0 / functools.reduce(lambda a, b: a + b, es)
        wts = [e * inv for e in es]
        for h in range(heads):
            sl = slice(h * HEAD_DIM, (h + 1) * HEAD_DIM)
            o = None
            for wt, o_ref in zip(wts, o_refs):
                t = wt[:, h:h + 1] * o_ref[:, sl]
                o = t if o is None else o + t
            g = g_ref[:, sl]
            y_fill[:, sl] = (o * _silu(g)).astype(y_fill.dtype)

        ssq = None
        for c in range(D // MXU_COLS):
            cols = slice(c * MXU_COLS, (c + 1) * MXU_COLS)
            x = x_ref[:, cols] + jnp.dot(y_use[...], w_ref[:, cols], preferred_element_type=F32)
            out_ref[:, cols] = x
            sq = x * x
            for hh in range(MXU_COLS // LANES):
                part = sq[:, hh * LANES:(hh + 1) * LANES]
                ssq = part if ssq is None else ssq + part
        scale = lax.rsqrt(jnp.sum(ssq, axis=-1, keepdims=True) * (1.0 / D) + RMS_EPS)
        out_ref[...] = out_ref[...] * scale * nw_ref[...]

    @pl.when(s % 2 == 0)
    def _():
        step(ya_ref, yb_ref)

    @pl.when(s % 2 == 1)
    def _():
        step(yb_ref, ya_ref)


def _mix_out(os_, lses, gate, w, x, norm_w, *, heads, tm):
    M, D = x.shape
    width = heads * HEAD_DIM
    n = len(os_)
    n_tiles = M // tm
    fill = lambda c: pl.BlockSpec((tm, c), lambda s: (jnp.minimum(s, n_tiles - 1), 0))
    use = lambda c: pl.BlockSpec((tm, c), lambda s: (jnp.maximum(s - 1, 0), 0))
    return pl.pallas_call(
        functools.partial(_mix_out_kernel, n_groups=n, heads=heads),
        grid=(n_tiles + 1,),
        in_specs=[fill(width)] * n + [fill(LANES)] * n + [
            fill(width),
            pl.BlockSpec((width, D), lambda s: (0, 0), pipeline_mode=pl.Buffered(1)),
            use(D),
            pl.BlockSpec((1, D), lambda s: (0, 0))],
        out_specs=use(D),
        out_shape=jax.ShapeDtypeStruct((M, D), F32),
        scratch_shapes=[pltpu.VMEM((tm, width), BF16), pltpu.VMEM((tm, width), BF16)],
        compiler_params=_params("arbitrary"),
        name="mix_out_norm",
    )(*os_, *lses, gate, w, x, norm_w.reshape(1, D))


def _rope_tables(S):
    half = ROPE_DIM // 2
    inv_freq = jnp.power(ROPE_THETA, -jnp.arange(half, dtype=F32) * (2.0 / ROPE_DIM))
    ang = jnp.arange(S).astype(F32)[:, None] * inv_freq[None, :]
    cos, sin = jnp.cos(ang), jnp.sin(ang)
    zeros = jnp.zeros((S, LANES - ROPE_DIM), F32)
    z16 = jnp.zeros((S, half), F32)
    return (jnp.concatenate([cos, cos, 1.0 + zeros], axis=1),
            jnp.concatenate([z16, sin, zeros], axis=1),
            jnp.concatenate([-sin, z16, zeros], axis=1))


def _deinterleave(t, dil):
    S, C = t.shape
    return t.reshape(S // dil, dil, C).transpose(1, 0, 2).reshape(S, C)


def _forward(x, norm_w, final_norm_w, na_w_in, na_rpb, na_w_out, dl_w_in, dl_w_out, *,
             tm_norm=512, tm_norm1=256, tm=1024, tm_plain=1024, tn_qkv0=768, tn=512, tm_out=256):
    B, S, D = x.shape
    M = B * S
    assert norm_w.shape[0] == 2 and na_w_in.shape[0] == 1 and dl_w_in.shape[0] == 1
    na_width = na_w_out.shape[1]
    na_heads = na_width // HEAD_DIM
    dl_width = dl_w_out.shape[1]
    dl_heads = dl_width // HEAD_DIM
    n_groups = len(DL_GROUPS)

    (xn0,) = _rmsnorm(x, norm_w[0], (1,), tm_norm)
    xn0 = xn0.reshape(M, D)
    (qkv0,) = _proj(xn0, na_w_in[0], [_Seg(0, 3 * na_width, tn_qkv0, BF16, n_q_cols=na_width, rot=(GRID_W, NA_ROT))],
                    tm=tm_plain,
                    name="proj0_qkv")
    (gate0,) = _proj(xn0, na_w_in[0], [_Seg(3 * na_width, na_width, tn, F32)], tm=tm_plain, name="proj0_gate")
    y0 = _na_attention(qkv0.reshape(B, S, 3 * na_width), gate0.reshape(B, S, na_width), na_rpb[0],
                       heads=na_heads)
    (x1,) = _proj(y0.reshape(M, na_width), na_w_out[0], [_Seg(0, D, tn, F32, mode="res")], tm=tm_plain,
                  res=x.reshape(M, D), name="out0")

    w_in1 = dl_w_in[0]
    dils = tuple(sorted({d for _, d in DL_GROUPS}))
    xn1 = dict(zip(dils, _rmsnorm(x1.reshape(B, S, D), norm_w[1], dils, tm_norm1)))
    rope = _rope_tables(S)
    os_, lses = [], []
    for gi, (window, dil) in enumerate(DL_GROUPS):
        half = window // (2 * dil)
        off = gi * 3 * dl_width
        qk, v = _proj(xn1[dil].reshape(M, D), w_in1,
                      [_Seg(off, 2 * dl_width, tn, BF16, mode="rope", n_q_cols=dl_width),
                       _Seg(off + 2 * dl_width, dl_width, tn // 2, BF16)],
                      tm=tm, rope=[_deinterleave(t, dil) for t in rope], name=f"proj1_g{gi}")
        o, lse = _dilated_attention(qk.reshape(B, dil, S // dil, 2 * dl_width),
                                    v.reshape(B, dil, S // dil, dl_width), B=B, dil=dil, half=half,
                                    heads=dl_heads)
        os_.append(o.reshape(M, dl_width))
        lses.append(lse.reshape(M, LANES))
    (gate1,) = _proj(xn1[1].reshape(M, D), w_in1, [_Seg(3 * n_groups * dl_width, dl_width, tn, F32)], tm=tm_plain,
                     name="proj1_gate")
    out = _mix_out(os_, lses, gate1, dl_w_out[0].astype(BF16), x1, final_norm_w, heads=dl_heads, tm=tm_out)
    return out.reshape(B, S, D)


def kernel(x, norm_w, final_norm_w, na_w_in, na_rpb, na_w_out, dl_w_in, dl_w_out):
    return _forward(x, norm_w, final_norm_w, na_w_in, na_rpb, na_w_out, dl_w_in, dl_w_out)
```

```python
import functools
import math
from typing import Any, NamedTuple

import jax
import jax.numpy as jnp
from jax import lax
from jax.experimental import pallas as pl
from jax.experimental.pallas import tpu as pltpu

HEAD_DIM = 128
GRID_W = 64
NA_KH = 8
NA_KW = 16
DL_GROUPS = ((128, 1), (512, 4), (2048, 16))
ROPE_THETA = 500000.0
ROPE_DIM = HEAD_DIM // 4
RMS_EPS = 1e-6
NEG_INF = -1e30
SCALE = HEAD_DIM ** -0.5
LOG2E = math.log2(math.e)
LN2 = math.log(2.0)
Q_SCALE = SCALE * LOG2E

LANES = 128
MXU_COLS = 256
VMEM_LIMIT_BYTES = 56 * 1024 * 1024

BF16 = jnp.bfloat16
F32 = jnp.float32


def _params(*semantics):
    return pltpu.CompilerParams(dimension_semantics=semantics, vmem_limit_bytes=VMEM_LIMIT_BYTES)


def _silu(g):
    hg = 0.5 * g
    return hg + hg * jnp.tanh(hg)


def _attend(blocks):
    logits = [lax.dot_general(q, k, (((1,), (1,)), ((), ())), preferred_element_type=F32) + bias
              for q, k, _, bias in blocks]
    out = []
    for s, (_, _, v, _) in zip(logits, blocks):
        m = jnp.max(s, axis=-1, keepdims=True)
        p = jnp.exp2(s - m).astype(BF16)
        v_aug = jnp.concatenate([v, jnp.ones_like(v)], axis=1)
        pv = jnp.dot(p, v_aug, preferred_element_type=F32)
        den = pv[:, HEAD_DIM:]
        out.append((pv[:, :HEAD_DIM] / den, m, den))
    return out


def _rmsnorm_kernel(x_ref, g_ref, *rest, dils):
    out_refs, y_ref = rest[:-1], rest[-1]
    x = x_ref[0]
    y = x * lax.rsqrt(jnp.mean(x * x, axis=-1, keepdims=True) + RMS_EPS)
    y = y * g_ref[...]
    tm, D = x.shape
    n_chunks = D // LANES
    if any(d > 1 for d in dils):
        for c in range(n_chunks):
            y_ref[c] = y[:, c * LANES:(c + 1) * LANES]
    for d, o_ref in zip(dils, out_refs):
        if d == 1:
            o_ref[0, 0] = y.astype(o_ref.dtype)
        else:
            for c in range(n_chunks):
                for r in range(d):
                    o_ref[0, r, :, c * LANES:(c + 1) * LANES] = (
                        y_ref[c, pl.ds(r, tm // d, stride=d), :].astype(o_ref.dtype))


def _rmsnorm(x, g, dils, tm):
    B, S, D = x.shape
    out_shape = [jax.ShapeDtypeStruct((B, d, S // d, D), BF16) for d in dils]
    out_specs = [pl.BlockSpec((1, d, tm // d, D), lambda b, i: (b, 0, i, 0)) for d in dils]
    return pl.pallas_call(
        functools.partial(_rmsnorm_kernel, dils=dils),
        grid=(B, S // tm),
        in_specs=[pl.BlockSpec((1, tm, D), lambda b, i: (b, i, 0)),
                  pl.BlockSpec((1, D), lambda b, i: (0, 0))],
        out_specs=out_specs,
        out_shape=out_shape,
        scratch_shapes=[pltpu.VMEM((D // LANES, tm, LANES), F32)],
        compiler_params=_params("parallel", "parallel"),
        name="rmsnorm",
    )(x, g.reshape(1, D))


CAST_ROWS = 256


class _Seg(NamedTuple):
    col_off: int
    n_cols: int
    tn: int
    out_dtype: Any
    mode: str = "plain"
    n_q_cols: int = 0
    rot: Any = None


def _proj_kernel(a_ref, *refs, segs, has_res, has_rope):
    n = len(segs)
    w_refs, refs = refs[:n], refs[n:]
    if has_res:
        res_ref, refs = refs[0], refs[1:]
    if has_rope:
        (cos_ref, shi_ref, slo_ref), refs = refs[:3], refs[3:]
    o_refs, wb_refs = refs[:n], refs[n:]
    j = pl.program_id(0)

    @pl.when(pl.program_id(1) == 0)
    def _convert_weights():
        for w_ref, wb_ref in zip(w_refs, wb_refs):
            def body(r, carry, w_ref=w_ref, wb_ref=wb_ref):
                rows = pl.ds(pl.multiple_of(r * CAST_ROWS, CAST_ROWS), CAST_ROWS)
                wb_ref[rows, :] = w_ref[rows, :].astype(BF16)
                return carry
            lax.fori_loop(0, w_ref.shape[0] // CAST_ROWS, body, 0)

    half = ROPE_DIM // 2
    for seg, wb_ref, o_ref in zip(segs, wb_refs, o_refs):
        for c in range(seg.tn // MXU_COLS):
            cols = slice(c * MXU_COLS, (c + 1) * MXU_COLS)
            hm = a_ref.shape[0] // 2
            acc = jnp.concatenate(
                [jnp.dot(a_ref[0:hm, :], wb_ref[:, cols], preferred_element_type=F32),
                 jnp.dot(a_ref[hm:, :], wb_ref[:, cols], preferred_element_type=F32)], axis=0)
            if seg.mode == "res":
                acc = res_ref[:, cols] + acc
            elif seg.mode == "rope":
                fac = jnp.where(j * seg.tn < seg.n_q_cols, Q_SCALE, 1.0).astype(F32)
                heads = []
                for hh in range(MXU_COLS // LANES):
                    x = acc[:, hh * LANES:(hh + 1) * LANES] * fac
                    heads.append(x * cos_ref[...] + pltpu.roll(x, half, 1) * shi_ref[...]
                                 + pltpu.roll(x, LANES - half, 1) * slo_ref[...])
                acc = jnp.concatenate(heads, axis=1)
            elif seg.n_q_cols:
                is_q = j * seg.tn + c * MXU_COLS < seg.n_q_cols
                if seg.rot:
                    acc = jnp.where(is_q, acc * Q_SCALE, _rot_rows(acc, *seg.rot))
                else:
                    acc = acc * jnp.where(is_q, Q_SCALE, 1.0).astype(F32)
            o_ref[:, cols] = acc.astype(o_ref.dtype)


def _proj(a, w, segs, *, tm, res=None, rope=None, name):
    M, K = a.shape
    steps = segs[0].n_cols // segs[0].tn
    assert M % tm == 0 and K % CAST_ROWS == 0
    in_specs = [pl.BlockSpec((tm, K), lambda j, i: (i, 0))]
    args = [a]
    for seg in segs:
        assert seg.n_cols == steps * seg.tn and seg.col_off % seg.tn == 0 and seg.tn % MXU_COLS == 0
        in_specs.append(pl.BlockSpec((K, seg.tn), lambda j, i, off=seg.col_off // seg.tn: (0, j + off)))
        args.append(w)
    if res is not None:
        assert len(segs) == 1 and segs[0].mode == "res"
        in_specs.append(pl.BlockSpec((tm, segs[0].tn), lambda j, i: (i, j)))
        args.append(res)
    if rope is not None:
        (seg,) = [s for s in segs if s.mode == "rope"]
        P = rope[0].shape[0]
        assert P % tm == 0 and seg.n_q_cols % seg.tn == 0
        nper = P // tm
        for t in rope:
            in_specs.append(pl.BlockSpec((tm, LANES), lambda j, i: (i % nper, 0)))
            args.append(t)
    return pl.pallas_call(
        functools.partial(_proj_kernel, segs=tuple(segs), has_res=res is not None, has_rope=rope is not None),
        grid=(steps, M // tm),
        in_specs=in_specs,
        out_specs=[pl.BlockSpec((tm, seg.tn), lambda j, i: (i, j)) for seg in segs],
        out_shape=[jax.ShapeDtypeStruct((M, seg.n_cols), seg.out_dtype) for seg in segs],
        scratch_shapes=[pltpu.VMEM((K, seg.tn), BF16) for seg in segs],
        compiler_params=_params("arbitrary", "arbitrary"),
        name=name,
    )(*args)


NA_QROWS = 8
NA_KROWS = 16
NA_QCOLS = 16
NA_KCOLS = 32
NA_ROT = NA_KW // 2
NA_RB = 8


def _rot_rows(x, group, shift):
    pieces = []
    for g in range(x.shape[0] // group):
        blk = x[g * group:(g + 1) * group]
        pieces += [blk[group - shift:], blk[:group - shift]]
    return jnp.concatenate(pieces, axis=0)


def _na_block_tables(rows):
    nblk = rows // NA_QROWS
    kb0s, vids, variants = [], [], []
    for i in range(nblk):
        kb0 = min(max(NA_QROWS * i - NA_KH // 2, 0), rows - NA_KROWS)
        tab = []
        for a in range(NA_QROWS):
            r = NA_QROWS * i + a
            rs = min(max(r - NA_KH // 2, 0), rows - NA_KH)
            for kl in range(NA_KROWS):
                kr = kb0 + kl
                tab.append(kr - r + NA_KH - 1 if rs <= kr < rs + NA_KH else None)
        tab = tuple(tab)
        if tab not in variants:
            variants.append(tab)
        kb0s.append(kb0)
        vids.append(variants.index(tab))
    return kb0s, vids, variants


def _na_kernel(rpb_ref, kb0_ref, vid_ref, q_ref, k_ref, v_ref, g_ref, o_ref, tile_ref, bias_ref,
               *, variants, nblk):
    h = pl.program_id(0)
    W = GRID_W
    n_ct = W // NA_QCOLS
    n_dr, n_dc = 2 * NA_KH - 1, 2 * NA_KW - 1
    per = LANES // NA_KCOLS

    @pl.when(pl.program_id(1) == 0)
    def _build_bias():
        qcl = lax.broadcasted_iota(jnp.int32, (NA_QCOLS, LANES), 0)
        lane = lax.broadcasted_iota(jnp.int32, (NA_QCOLS, LANES), 1)
        kcl = lane & (NA_KCOLS - 1)
        for ct in range(n_ct):
            qc = ct * NA_QCOLS + qcl
            kc = (ct * NA_QCOLS + kcl - NA_ROT) & (W - 1)
            cs = jnp.clip(qc - NA_KW // 2, 0, W - NA_KW)
            col_ok = (kc >= cs) & (kc < cs + NA_KW)
            dc = jnp.clip(kc - qc + NA_KW - 1, 0, n_dc - 1)
            for dr in range(n_dr):
                table = jnp.broadcast_to(rpb_ref[0, dr:dr + 1, :] * LOG2E, (NA_QCOLS, LANES))
                tile_ref[ct, dr] = jnp.where(col_ok, jnp.take_along_axis(table, dc, axis=1), NEG_INF)
        neg = jnp.full((NA_QCOLS, LANES), NEG_INF, F32)
        part = lane // NA_KCOLS
        for vi, tab in enumerate(variants):
            for ct in range(n_ct):
                for a in range(NA_QROWS):
                    for m in range(NA_KROWS // per):
                        piece = neg
                        for i in range(per):
                            d = tab[a * NA_KROWS + per * m + i]
                            if d is not None:
                                piece = jnp.where(part == i, tile_ref[ct, d], piece)
                        bias_ref[vi, ct, a * NA_QCOLS:(a + 1) * NA_QCOLS, m * LANES:(m + 1) * LANES] = piece

    def rows_of(ref, row0, n_rows, cols):
        band = ref.at[0, pl.ds(pl.multiple_of(row0 * W, W), n_rows * W), :]
        return jnp.concatenate(
            [band[a * W + c:a * W + c + NA_QCOLS, :] for a in range(n_rows) for c in cols], axis=0)

    def load(i, ct):
        kpos = [(ct * NA_QCOLS + u * NA_QCOLS) % W for u in range(NA_KCOLS // NA_QCOLS)]
        return (rows_of(q_ref, i * NA_QROWS, NA_QROWS, [ct * NA_QCOLS]),
                rows_of(k_ref, kb0_ref[i], NA_KROWS, kpos),
                rows_of(v_ref, kb0_ref[i], NA_KROWS, kpos),
                bias_ref[vid_ref[i], ct])

    def group(n, carry):
        tiles = [(n * NA_RB + rb, ct) for rb in range(NA_RB) for ct in range(n_ct)]
        for (i, ct), (o, _, _) in zip(tiles, _attend([load(i, ct) for i, ct in tiles])):
            g = rows_of(g_ref, i * NA_QROWS, NA_QROWS, [ct * NA_QCOLS])
            y = (o * _silu(g)).astype(o_ref.dtype)
            band = o_ref.at[0, pl.ds(pl.multiple_of(i * NA_QROWS * W, W), NA_QROWS * W), :]
            for a in range(NA_QROWS):
                band[a * W + ct * NA_QCOLS:a * W + (ct + 1) * NA_QCOLS, :] = y[a * NA_QCOLS:(a + 1) * NA_QCOLS, :]
        return carry

    lax.fori_loop(0, nblk // NA_RB, group, 0)


def _na_attention(qkv, gate, rpb, *, heads):
    B, S, _ = qkv.shape
    rows = S // GRID_W
    assert rows % (NA_RB * NA_QROWS) == 0 and rows >= NA_KROWS
    kb0s, vids, variants = _na_block_tables(rows)
    nblk = rows // NA_QROWS
    n_ct = GRID_W // NA_QCOLS
    smem = pl.BlockSpec(memory_space=pltpu.SMEM)
    tok = lambda off: pl.BlockSpec((1, S, HEAD_DIM), lambda h, b: (b, 0, h + off))
    n_dr, n_dc = rpb.shape[1:]
    rpb_rows = -(-n_dr // 8) * 8
    rpb = jnp.pad(rpb, ((0, 0), (0, rpb_rows - n_dr), (0, LANES - n_dc)))
    return pl.pallas_call(
        functools.partial(_na_kernel, variants=variants, nblk=nblk),
        grid=(heads, B),
        in_specs=[pl.BlockSpec((1, rpb_rows, LANES), lambda h, b: (h, 0, 0)), smem, smem,
                  tok(0), tok(heads), tok(2 * heads), tok(0)],
        out_specs=tok(0),
        out_shape=jax.ShapeDtypeStruct((B, S, heads * HEAD_DIM), BF16),
        scratch_shapes=[pltpu.VMEM((n_ct, 2 * NA_KH - 1, NA_QCOLS, LANES), F32),
                        pltpu.VMEM((len(variants), n_ct, NA_QROWS * NA_QCOLS, NA_KROWS * NA_KCOLS), F32)],
        compiler_params=_params("arbitrary", "arbitrary"),
        name="na_attention",
    )(rpb, jnp.asarray(kb0s, jnp.int32), jnp.asarray(vids, jnp.int32), qkv, qkv, qkv, gate)


DL_QB = 128
DL_UNROLL = 32
DL_MAX_STRIDE = 4


def _dilated_kernel(q_ref, k_ref, v_ref, o_ref, lse_ref, mask_ref, *stage_refs, dil, half, L):
    h = pl.program_id(1)
    qb = min(DL_QB, L - 2 * half)
    kb = qb + 2 * half
    nqb = L // qb
    d_in = min(dil, DL_MAX_STRIDE)
    d_out = dil // d_in

    @pl.when(h == 0)
    def _():
        lse_ref[...] = jnp.zeros_like(lse_ref)

    @pl.when((pl.program_id(0) == 0) & (h == 0))
    def _():
        rel = (lax.broadcasted_iota(jnp.int32, (qb, kb), 1) - lax.broadcasted_iota(jnp.int32, (qb, kb), 0))
        for vi in range(3):
            mask_ref[vi] = jnp.where(jnp.abs(rel - vi * half) <= half, 0.0, NEG_INF)

    def load(r, q0):
        k0 = pl.multiple_of(jnp.clip(q0 - half, 0, L - kb), half)
        return (q_ref[0, r, pl.ds(q0, qb), :], k_ref[0, r, pl.ds(k0, kb), :], v_ref[0, r, pl.ds(k0, kb), :],
                mask_ref[(q0 - k0) // half])

    def put_lse(rows, lse):
        lane = lax.broadcasted_iota(jnp.int32, lse.shape, 1)
        lse_ref[0, rows, :] = jnp.where(lane == h, lse, lse_ref[0, rows, :])

    if d_out == 1:
        unroll = math.gcd(dil * nqb, DL_UNROLL)

        def group(t, carry):
            blocks = [((t * unroll + u) // nqb, pl.multiple_of(((t * unroll + u) % nqb) * qb, qb))
                      for u in range(unroll)]
            for (r, q0), (o, m, den) in zip(blocks, _attend([load(r, q0) for r, q0 in blocks])):
                rows = pl.ds(q0, qb) if dil == 1 else pl.ds(q0 * dil + r, qb, stride=dil)
                o_ref[0, rows, :] = o
                put_lse(rows, m * LN2 + jnp.log(den))
            return carry

        lax.fori_loop(0, dil * nqb // unroll, group, 0)
    else:
        o_tmp, l_tmp = stage_refs

        def group(r0, carry):
            blocks = [(qi, r1) for qi in range(nqb) for r1 in range(d_out)]
            outs = _attend([load(r1 * d_in + r0, qi * qb) for qi, r1 in blocks])
            for (qi, r1), (o, m, den) in zip(blocks, outs):
                hop1 = pl.ds(r1, qb, stride=d_out)
                o_tmp[qi, hop1, :] = o
                l_tmp[qi, hop1, :] = m * LN2 + jnp.log(den)
            for qi in range(nqb):
                hop2 = pl.ds(qi * qb * dil + r0, qb * d_out, stride=d_in)
                o_ref[0, hop2, :] = o_tmp[qi]
                put_lse(hop2, l_tmp[qi])
            return carry

        lax.fori_loop(0, d_in, group, 0)


def _dilated_attention(qk, v, *, B, dil, half, heads):
    L = qk.shape[2]
    S = L * dil
    qb = min(DL_QB, L - 2 * half)
    assert L % qb == 0 and heads <= LANES
    d_out = dil // min(dil, DL_MAX_STRIDE)
    assert dil % min(dil, DL_MAX_STRIDE) == 0
    scratch = [pltpu.VMEM((3, qb, qb + 2 * half), F32)]
    if d_out > 1:
        scratch += [pltpu.VMEM((L // qb, qb * d_out, LANES), F32)] * 2
    blk = lambda off: pl.BlockSpec((1, dil, L, HEAD_DIM), lambda b, h: (b, 0, 0, h + off))
    return pl.pallas_call(
        functools.partial(_dilated_kernel, dil=dil, half=half, L=L),
        grid=(B, heads),
        in_specs=[blk(0), blk(heads), blk(0)],
        out_specs=[pl.BlockSpec((1, S, HEAD_DIM), lambda b, h: (b, 0, h)),
                   pl.BlockSpec((1, S, LANES), lambda b, h: (b, 0, 0))],
        out_shape=[jax.ShapeDtypeStruct((B, S, heads * HEAD_DIM), F32),
                   jax.ShapeDtypeStruct((B, S, LANES), F32)],
        scratch_shapes=scratch,
        compiler_params=_params("arbitrary", "arbitrary"),
        name=f"dilated_attention_d{dil}",
    )(qk, qk, v)


def _mix_out_kernel(*refs, n_groups, heads):
    o_refs = refs[:n_groups]
    l_refs = refs[n_groups:2 * n_groups]
    g_ref, w_ref, x_ref, nw_ref, out_ref, ya_ref, yb_ref = refs[2 * n_groups:]
    s = pl.program_id(0)
    D = out_ref.shape[1]

    @pl.when(s == 0)
    def _():
        yb_ref[...] = jnp.zeros_like(yb_ref)

    def step(y_fill, y_use):
        lses = [r[...] for r in l_refs]
        mx = functools.reduce(jnp.maximum, lses)
        es = [jnp.exp(l - mx) for l in lses]
        inv = 1.0 / functools.reduce(lambda a, b: a + b, es)
        wts = [e * inv for e in es]
        for h in range(heads):
            sl = slice(h * HEAD_DIM, (h + 1) * HEAD_DIM)
            o = None
            for wt, o_ref in zip(wts, o_refs):
                t = wt[:, h:h + 1] * o_ref[:, sl]
                o = t if o is None else o + t
            g = g_ref[:, sl]
            y_fill[:, sl] = (o * _silu(g)).astype(y_fill.dtype)

        ssq = None
        for c in range(D // MXU_COLS):
            cols = slice(c * MXU_COLS, (c + 1) * MXU_COLS)
            x = x_ref[:, cols] + jnp.dot(y_use[...], w_ref[:, cols], preferred_element_type=F32)
            out_ref[:, cols] = x
            sq = x * x
            for hh in range(MXU_COLS // LANES):
                part = sq[:, hh * LANES:(hh + 1) * LANES]
                ssq = part if ssq is None else ssq + part
        scale = lax.rsqrt(jnp.sum(ssq, axis=-1, keepdims=True) * (1.0 / D) + RMS_EPS)
        out_ref[...] = out_ref[...] * scale * nw_ref[...]

    @pl.when(s % 2 == 0)
    def _():
        step(ya_ref, yb_ref)

    @pl.when(s % 2 == 1)
    def _():
        step(yb_ref, ya_ref)


def _mix_out(os_, lses, gate, w, x, norm_w, *, heads, tm):
    M, D = x.shape
    width = heads * HEAD_DIM
    n = len(os_)
    n_tiles = M // tm
    fill = lambda c: pl.BlockSpec((tm, c), lambda s: (jnp.minimum(s, n_tiles - 1), 0))
    use = lambda c: pl.BlockSpec((tm, c), lambda s: (jnp.maximum(s - 1, 0), 0))
    return pl.pallas_call(
        functools.partial(_mix_out_kernel, n_groups=n, heads=heads),
        grid=(n_tiles + 1,),
        in_specs=[fill(width)] * n + [fill(LANES)] * n + [
            fill(width),
            pl.BlockSpec((width, D), lambda s: (0, 0), pipeline_mode=pl.Buffered(1)),
            use(D),
            pl.BlockSpec((1, D), lambda s: (0, 0))],
        out_specs=use(D),
        out_shape=jax.ShapeDtypeStruct((M, D), F32),
        scratch_shapes=[pltpu.VMEM((tm, width), BF16), pltpu.VMEM((tm, width), BF16)],
        compiler_params=_params("arbitrary"),
        name="mix_out_norm",
    )(*os_, *lses, gate, w, x, norm_w.reshape(1, D))


def _rope_tables(S):
    half = ROPE_DIM // 2
    inv_freq = jnp.power(ROPE_THETA, -jnp.arange(half, dtype=F32) * (2.0 / ROPE_DIM))
    ang = jnp.arange(S).astype(F32)[:, None] * inv_freq[None, :]
    cos, sin = jnp.cos(ang), jnp.sin(ang)
    zeros = jnp.zeros((S, LANES - ROPE_DIM), F32)
    z16 = jnp.zeros((S, half), F32)
    return (jnp.concatenate([cos, cos, 1.0 + zeros], axis=1),
            jnp.concatenate([z16, sin, zeros], axis=1),
            jnp.concatenate([-sin, z16, zeros], axis=1))


def _deinterleave(t, dil):
    S, C = t.shape
    return t.reshape(S // dil, dil, C).transpose(1, 0, 2).reshape(S, C)


def _forward(x, norm_w, final_norm_w, na_w_in, na_rpb, na_w_out, dl_w_in, dl_w_out, *,
             tm_norm=512, tm_norm1=256, tm=1024, tm_plain=1024, tn_qkv0=768, tn=512, tm_out=256):
    B, S, D = x.shape
    M = B * S
    assert norm_w.shape[0] == 2 and na_w_in.shape[0] == 1 and dl_w_in.shape[0] == 1
    na_width = na_w_out.shape[1]
    na_heads = na_width // HEAD_DIM
    dl_width = dl_w_out.shape[1]
    dl_heads = dl_width // HEAD_DIM
    n_groups = len(DL_GROUPS)

    (xn0,) = _rmsnorm(x, norm_w[0], (1,), tm_norm)
    xn0 = xn0.reshape(M, D)
    (qkv0,) = _proj(xn0, na_w_in[0], [_Seg(0, 3 * na_width, tn_qkv0, BF16, n_q_cols=na_width, rot=(GRID_W, NA_ROT))],
                    tm=tm_plain,
                    name="proj0_qkv")
    (gate0,) = _proj(xn0, na_w_in[0], [_Seg(3 * na_width, na_width, tn, F32)], tm=tm_plain, name="proj0_gate")
    y0 = _na_attention(qkv0.reshape(B, S, 3 * na_width), gate0.reshape(B, S, na_width), na_rpb[0],
                       heads=na_heads)
    (x1,) = _proj(y0.reshape(M, na_width), na_w_out[0], [_Seg(0, D, tn, F32, mode="res")], tm=tm_plain,
                  res=x.reshape(M, D), name="out0")

    w_in1 = dl_w_in[0]
    dils = tuple(sorted({d for _, d in DL_GROUPS}))
    xn1 = dict(zip(dils, _rmsnorm(x1.reshape(B, S, D), norm_w[1], dils, tm_norm1)))
    rope = _rope_tables(S)
    os_, lses = [], []
    for gi, (window, dil) in enumerate(DL_GROUPS):
        half = window // (2 * dil)
        off = gi * 3 * dl_width
        qk, v = _proj(xn1[dil].reshape(M, D), w_in1,
                      [_Seg(off, 2 * dl_width, tn, BF16, mode="rope", n_q_cols=dl_width),
                       _Seg(off + 2 * dl_width, dl_width, tn // 2, BF16)],
                      tm=tm, rope=[_deinterleave(t, dil) for t in rope], name=f"proj1_g{gi}")
        o, lse = _dilated_attention(qk.reshape(B, dil, S // dil, 2 * dl_width),
                                    v.reshape(B, dil, S // dil, dl_width), B=B, dil=dil, half=half,
                                    heads=dl_heads)
        os_.append(o.reshape(M, dl_width))
        lses.append(lse.reshape(M, LANES))
    (gate1,) = _proj(xn1[1].reshape(M, D), w_in1, [_Seg(3 * n_groups * dl_width, dl_width, tn, F32)], tm=tm_plain,
                     name="proj1_gate")
    out = _mix_out(os_, lses, gate1, dl_w_out[0].astype(BF16), x1, final_norm_w, heads=dl_heads, tm=tm_out)
    return out.reshape(B, S, D)


def kernel(x, norm_w, final_norm_w, na_w_in, na_rpb, na_w_out, dl_w_in, dl_w_out):
    return _forward(x, norm_w, final_norm_w, na_w_in, na_rpb, na_w_out, dl_w_in, dl_w_out)
```

```python
import functools
import math
from typing import Any, NamedTuple

import jax
import jax.numpy as jnp
from jax import lax
from jax.experimental import pallas as pl
from jax.experimental.pallas import tpu as pltpu

HEAD_DIM = 128
GRID_W = 64
NA_KH = 8
NA_KW = 16
DL_GROUPS = ((128, 1), (512, 4), (2048, 16))
ROPE_THETA = 500000.0
ROPE_DIM = HEAD_DIM // 4
RMS_EPS = 1e-6
NEG_INF = -1e30
SCALE = HEAD_DIM ** -0.5
LOG2E = math.log2(math.e)
LN2 = math.log(2.0)
Q_SCALE = SCALE * LOG2E

LANES = 128
MXU_COLS = 256
VMEM_LIMIT_BYTES = 56 * 1024 * 1024

BF16 = jnp.bfloat16
F32 = jnp.float32


def _params(*semantics):
    return pltpu.CompilerParams(dimension_semantics=semantics, vmem_limit_bytes=VMEM_LIMIT_BYTES)


def _silu(g):
    hg = 0.5 * g
    return hg + hg * jnp.tanh(hg)


def _attend(blocks):
    logits = [lax.dot_general(q, k, (((1,), (1,)), ((), ())), preferred_element_type=F32) + bias
              for q, k, _, bias in blocks]
    out = []
    for s, (_, _, v, _) in zip(logits, blocks):
        m = jnp.max(s, axis=-1, keepdims=True)
        p = jnp.exp2(s - m).astype(BF16)
        v_aug = jnp.concatenate([v, jnp.ones_like(v)], axis=1)
        pv = jnp.dot(p, v_aug, preferred_element_type=F32)
        den = pv[:, HEAD_DIM:]
        out.append((pv[:, :HEAD_DIM] / den, m, den))
    return out


def _rmsnorm_kernel(x_ref, g_ref, *rest, dils):
    out_refs, y_ref = rest[:-1], rest[-1]
    x = x_ref[0]
    y = x * lax.rsqrt(jnp.mean(x * x, axis=-1, keepdims=True) + RMS_EPS)
    y = y * g_ref[...]
    tm, D = x.shape
    n_chunks = D // LANES
    if any(d > 1 for d in dils):
        for c in range(n_chunks):
            y_ref[c] = y[:, c * LANES:(c + 1) * LANES]
    for d, o_ref in zip(dils, out_refs):
        if d == 1:
            o_ref[0, 0] = y.astype(o_ref.dtype)
        else:
            for c in range(n_chunks):
                for r in range(d):
                    o_ref[0, r, :, c * LANES:(c + 1) * LANES] = (
                        y_ref[c, pl.ds(r, tm // d, stride=d), :].astype(o_ref.dtype))


def _rmsnorm(x, g, dils, tm):
    B, S, D = x.shape
    out_shape = [jax.ShapeDtypeStruct((B, d, S // d, D), BF16) for d in dils]
    out_specs = [pl.BlockSpec((1, d, tm // d, D), lambda b, i: (b, 0, i, 0)) for d in dils]
    return pl.pallas_call(
        functools.partial(_rmsnorm_kernel, dils=dils),
        grid=(B, S // tm),
        in_specs=[pl.BlockSpec((1, tm, D), lambda b, i: (b, i, 0)),
                  pl.BlockSpec((1, D), lambda b, i: (0, 0))],
        out_specs=out_specs,
        out_shape=out_shape,
        scratch_shapes=[pltpu.VMEM((D // LANES, tm, LANES), F32)],
        compiler_params=_params("parallel", "parallel"),
        name="rmsnorm",
    )(x, g.reshape(1, D))


CAST_ROWS = 256


class _Seg(NamedTuple):
    col_off: int
    n_cols: int
    tn: int
    out_dtype: Any
    mode: str = "plain"
    n_q_cols: int = 0
    rot: Any = None


def _proj_kernel(a_ref, *refs, segs, has_res, has_rope):
    n = len(segs)
    w_refs, refs = refs[:n], refs[n:]
    if has_res:
        res_ref, refs = refs[0], refs[1:]
    if has_rope:
        (cos_ref, sin_ref), refs = refs[:2], refs[2:]
    o_refs, wb_refs = refs[:n], refs[n:]
    j = pl.program_id(0)

    @pl.when(pl.program_id(1) == 0)
    def _convert_weights():
        for w_ref, wb_ref in zip(w_refs, wb_refs):
            def body(r, carry, w_ref=w_ref, wb_ref=wb_ref):
                rows = pl.ds(pl.multiple_of(r * CAST_ROWS, CAST_ROWS), CAST_ROWS)
                wb_ref[rows, :] = w_ref[rows, :].astype(BF16)
                return carry
            lax.fori_loop(0, w_ref.shape[0] // CAST_ROWS, body, 0)

    half = ROPE_DIM // 2
    for seg, wb_ref, o_ref in zip(segs, wb_refs, o_refs):
        for c in range(seg.tn // MXU_COLS):
            cols = slice(c * MXU_COLS, (c + 1) * MXU_COLS)
            hm = a_ref.shape[0] // 2
            acc = jnp.concatenate(
                [jnp.dot(a_ref[0:hm, :], wb_ref[:, cols], preferred_element_type=F32),
                 jnp.dot(a_ref[hm:, :], wb_ref[:, cols], preferred_element_type=F32)], axis=0)
            if seg.mode == "res":
                acc = res_ref[:, cols] + acc
            elif seg.mode == "rope":
                fac = jnp.where(j * seg.tn < seg.n_q_cols, Q_SCALE, 1.0).astype(F32)
                first_half = lax.broadcasted_iota(jnp.int32, (acc.shape[0], LANES), 1) < half
                heads = []
                for hh in range(MXU_COLS // LANES):
                    x = acc[:, hh * LANES:(hh + 1) * LANES] * fac
                    partner = jnp.where(first_half, pltpu.roll(x, LANES - half, 1), pltpu.roll(x, half, 1))
                    heads.append(x * cos_ref[...] + partner * sin_ref[...])
                acc = jnp.concatenate(heads, axis=1)
            elif seg.n_q_cols:
                is_q = j * seg.tn + c * MXU_COLS < seg.n_q_cols
                if seg.rot:
                    acc = jnp.where(is_q, acc * Q_SCALE, _rot_rows(acc, *seg.rot))
                else:
                    acc = acc * jnp.where(is_q, Q_SCALE, 1.0).astype(F32)
            o_ref[:, cols] = acc.astype(o_ref.dtype)


def _proj(a, w, segs, *, tm, res=None, rope=None, name):
    M, K = a.shape
    steps = segs[0].n_cols // segs[0].tn
    assert M % tm == 0 and K % CAST_ROWS == 0
    in_specs = [pl.BlockSpec((tm, K), lambda j, i: (i, 0))]
    args = [a]
    for seg in segs:
        assert seg.n_cols == steps * seg.tn and seg.col_off % seg.tn == 0 and seg.tn % MXU_COLS == 0
        in_specs.append(pl.BlockSpec((K, seg.tn), lambda j, i, off=seg.col_off // seg.tn: (0, j + off)))
        args.append(w)
    if res is not None:
        assert len(segs) == 1 and segs[0].mode == "res"
        in_specs.append(pl.BlockSpec((tm, segs[0].tn), lambda j, i: (i, j)))
        args.append(res)
    if rope is not None:
        (seg,) = [s for s in segs if s.mode == "rope"]
        P = rope[0].shape[0]
        assert P % tm == 0 and seg.n_q_cols % seg.tn == 0
        nper = P // tm
        for t in rope:
            in_specs.append(pl.BlockSpec((tm, LANES), lambda j, i: (i % nper, 0)))
            args.append(t)
    return pl.pallas_call(
        functools.partial(_proj_kernel, segs=tuple(segs), has_res=res is not None, has_rope=rope is not None),
        grid=(steps, M // tm),
        in_specs=in_specs,
        out_specs=[pl.BlockSpec((tm, seg.tn), lambda j, i: (i, j)) for seg in segs],
        out_shape=[jax.ShapeDtypeStruct((M, seg.n_cols), seg.out_dtype) for seg in segs],
        scratch_shapes=[pltpu.VMEM((K, seg.tn), BF16) for seg in segs],
        compiler_params=_params("arbitrary", "arbitrary"),
        name=name,
    )(*args)


NA_QROWS = 8
NA_KROWS = 16
NA_QCOLS = 16
NA_KCOLS = 32
NA_ROT = NA_KW // 2
NA_RB = 16


def _rot_rows(x, group, shift):
    pieces = []
    for g in range(x.shape[0] // group):
        blk = x[g * group:(g + 1) * group]
        pieces += [blk[group - shift:], blk[:group - shift]]
    return jnp.concatenate(pieces, axis=0)


def _na_block_tables(rows):
    nblk = rows // NA_QROWS
    kb0s, vids, variants = [], [], []
    for i in range(nblk):
        kb0 = min(max(NA_QROWS * i - NA_KH // 2, 0), rows - NA_KROWS)
        tab = []
        for a in range(NA_QROWS):
            r = NA_QROWS * i + a
            rs = min(max(r - NA_KH // 2, 0), rows - NA_KH)
            for kl in range(NA_KROWS):
                kr = kb0 + kl
                tab.append(kr - r + NA_KH - 1 if rs <= kr < rs + NA_KH else None)
        tab = tuple(tab)
        if tab not in variants:
            variants.append(tab)
        kb0s.append(kb0)
        vids.append(variants.index(tab))
    return kb0s, vids, variants


def _na_kernel(rpb_ref, kb0_ref, vid_ref, q_ref, k_ref, v_ref, g_ref, o_ref, tile_ref, bias_ref,
               *, variants, nblk):
    h = pl.program_id(0)
    W = GRID_W
    n_ct = W // NA_QCOLS
    n_dr, n_dc = 2 * NA_KH - 1, 2 * NA_KW - 1
    per = LANES // NA_KCOLS

    @pl.when(pl.program_id(1) == 0)
    def _build_bias():
        qcl = lax.broadcasted_iota(jnp.int32, (NA_QCOLS, LANES), 0)
        lane = lax.broadcasted_iota(jnp.int32, (NA_QCOLS, LANES), 1)
        kcl = lane & (NA_KCOLS - 1)
        for ct in range(n_ct):
            qc = ct * NA_QCOLS + qcl
            kc = (ct * NA_QCOLS + kcl - NA_ROT) & (W - 1)
            cs = jnp.clip(qc - NA_KW // 2, 0, W - NA_KW)
            col_ok = (kc >= cs) & (kc < cs + NA_KW)
            dc = jnp.clip(kc - qc + NA_KW - 1, 0, n_dc - 1)
            for dr in range(n_dr):
                table = jnp.broadcast_to(rpb_ref[0, dr:dr + 1, :] * LOG2E, (NA_QCOLS, LANES))
                tile_ref[ct, dr] = jnp.where(col_ok, jnp.take_along_axis(table, dc, axis=1), NEG_INF)
        neg = jnp.full((NA_QCOLS, LANES), NEG_INF, F32)
        part = lane // NA_KCOLS
        for vi, tab in enumerate(variants):
            for ct in range(n_ct):
                for a in range(NA_QROWS):
                    for m in range(NA_KROWS // per):
                        piece = neg
                        for i in range(per):
                            d = tab[a * NA_KROWS + per * m + i]
                            if d is not None:
                                piece = jnp.where(part == i, tile_ref[ct, d], piece)
                        bias_ref[vi, ct, a * NA_QCOLS:(a + 1) * NA_QCOLS, m * LANES:(m + 1) * LANES] = piece

    def rows_of(ref, row0, n_rows, cols):
        band = ref.at[0, pl.ds(pl.multiple_of(row0 * W, W), n_rows * W), :]
        return jnp.concatenate(
            [band[a * W + c:a * W + c + NA_QCOLS, :] for a in range(n_rows) for c in cols], axis=0)

    def load(i, ct):
        kpos = [(ct * NA_QCOLS + u * NA_QCOLS) % W for u in range(NA_KCOLS // NA_QCOLS)]
        return (rows_of(q_ref, i * NA_QROWS, NA_QROWS, [ct * NA_QCOLS]),
                rows_of(k_ref, kb0_ref[i], NA_KROWS, kpos),
                rows_of(v_ref, kb0_ref[i], NA_KROWS, kpos),
                bias_ref[vid_ref[i], ct])

    n_rb = math.gcd(nblk, NA_RB)

    def group(n, carry):
        tiles = [(n * n_rb + rb, ct) for rb in range(n_rb) for ct in range(n_ct)]
        for (i, ct), (o, _, _) in zip(tiles, _attend([load(i, ct) for i, ct in tiles])):
            g = rows_of(g_ref, i * NA_QROWS, NA_QROWS, [ct * NA_QCOLS])
            y = (o * _silu(g)).astype(o_ref.dtype)
            band = o_ref.at[0, pl.ds(pl.multiple_of(i * NA_QROWS * W, W), NA_QROWS * W), :]
            for a in range(NA_QROWS):
                band[a * W + ct * NA_QCOLS:a * W + (ct + 1) * NA_QCOLS, :] = y[a * NA_QCOLS:(a + 1) * NA_QCOLS, :]
        return carry

    lax.fori_loop(0, nblk // n_rb, group, 0)


def _na_attention(qkv, gate, rpb, *, heads):
    B, S, _ = qkv.shape
    rows = S // GRID_W
    assert rows % NA_QROWS == 0 and rows >= NA_KROWS
    kb0s, vids, variants = _na_block_tables(rows)
    nblk = rows // NA_QROWS
    n_ct = GRID_W // NA_QCOLS
    smem = pl.BlockSpec(memory_space=pltpu.SMEM)
    tok = lambda off: pl.BlockSpec((1, S, HEAD_DIM), lambda h, b: (b, 0, h + off))
    n_dr, n_dc = rpb.shape[1:]
    rpb_rows = -(-n_dr // 8) * 8
    rpb = jnp.pad(rpb, ((0, 0), (0, rpb_rows - n_dr), (0, LANES - n_dc)))
    return pl.pallas_call(
        functools.partial(_na_kernel, variants=variants, nblk=nblk),
        grid=(heads, B),
        in_specs=[pl.BlockSpec((1, rpb_rows, LANES), lambda h, b: (h, 0, 0)), smem, smem,
                  tok(0), tok(heads), tok(2 * heads), tok(0)],
        out_specs=tok(0),
        out_shape=jax.ShapeDtypeStruct((B, S, heads * HEAD_DIM), BF16),
        scratch_shapes=[pltpu.VMEM((n_ct, 2 * NA_KH - 1, NA_QCOLS, LANES), F32),
                        pltpu.VMEM((len(variants), n_ct, NA_QROWS * NA_QCOLS, NA_KROWS * NA_KCOLS), F32)],
        compiler_params=_params("arbitrary", "arbitrary"),
        name="na_attention",
    )(rpb, jnp.asarray(kb0s, jnp.int32), jnp.asarray(vids, jnp.int32), qkv, qkv, qkv, gate)


DL_QB = 128
DL_UNROLL = 32
DL_MAX_STRIDE = 4


def _dilated_kernel(q_ref, k_ref, v_ref, o_ref, lse_ref, mask_ref, *stage_refs, dil, half, L):
    h = pl.program_id(1)
    qb = min(DL_QB, L - 2 * half)
    kb = qb + 2 * half
    nqb = L // qb
    d_in = min(dil, DL_MAX_STRIDE)
    d_out = dil // d_in

    @pl.when(h == 0)
    def _():
        lse_ref[...] = jnp.zeros_like(lse_ref)

    @pl.when((pl.program_id(0) == 0) & (h == 0))
    def _():
        rel = (lax.broadcasted_iota(jnp.int32, (qb, kb), 1) - lax.broadcasted_iota(jnp.int32, (qb, kb), 0))
        for vi in range(3):
            mask_ref[vi] = jnp.where(jnp.abs(rel - vi * half) <= half, 0.0, NEG_INF)

    def load(r, q0):
        k0 = pl.multiple_of(jnp.clip(q0 - half, 0, L - kb), half)
        return (q_ref[0, r, pl.ds(q0, qb), :], k_ref[0, r, pl.ds(k0, kb), :], v_ref[0, r, pl.ds(k0, kb), :],
                mask_ref[(q0 - k0) // half])

    def put_lse(rows, lse):
        lane = lax.broadcasted_iota(jnp.int32, lse.shape, 1)
        lse_ref[0, rows, :] = jnp.where(lane == h, lse, lse_ref[0, rows, :])

    if d_out == 1:
        unroll = math.gcd(dil * nqb, DL_UNROLL)

        def group(t, carry):
            blocks = [((t * unroll + u) // nqb, pl.multiple_of(((t * unroll + u) % nqb) * qb, qb))
                      for u in range(unroll)]
            for (r, q0), (o, m, den) in zip(blocks, _attend([load(r, q0) for r, q0 in blocks])):
                rows = pl.ds(q0, qb) if dil == 1 else pl.ds(q0 * dil + r, qb, stride=dil)
                o_ref[0, rows, :] = o
                put_lse(rows, m * LN2 + jnp.log(den))
            return carry

        lax.fori_loop(0, dil * nqb // unroll, group, 0)
    else:
        o_tmp, l_tmp = stage_refs

        def group(r0, carry):
            blocks = [(qi, r1) for qi in range(nqb) for r1 in range(d_out)]
            outs = _attend([load(r1 * d_in + r0, qi * qb) for qi, r1 in blocks])
            for (qi, r1), (o, m, den) in zip(blocks, outs):
                hop1 = pl.ds(r1, qb, stride=d_out)
                o_tmp[qi, hop1, :] = o
                l_tmp[qi, hop1, :] = m * LN2 + jnp.log(den)
            for qi in range(nqb):
                hop2 = pl.ds(qi * qb * dil + r0, qb * d_out, stride=d_in)
                o_ref[0, hop2, :] = o_tmp[qi]
                put_lse(hop2, l_tmp[qi])
            return carry

        lax.fori_loop(0, d_in, group, 0)


def _dilated_attention(qk, v, *, B, dil, half, heads):
    L = qk.shape[2]
    S = L * dil
    qb = min(DL_QB, L - 2 * half)
    assert L % qb == 0 and heads <= LANES
    d_out = dil // min(dil, DL_MAX_STRIDE)
    assert dil % min(dil, DL_MAX_STRIDE) == 0
    scratch = [pltpu.VMEM((3, qb, qb + 2 * half), F32)]
    if d_out > 1:
        scratch += [pltpu.VMEM((L // qb, qb * d_out, LANES), F32)] * 2
    blk = lambda off: pl.BlockSpec((1, dil, L, HEAD_DIM), lambda b, h: (b, 0, 0, h + off))
    return pl.pallas_call(
        functools.partial(_dilated_kernel, dil=dil, half=half, L=L),
        grid=(B, heads),
        in_specs=[blk(0), blk(heads), blk(0)],
        out_specs=[pl.BlockSpec((1, S, HEAD_DIM), lambda b, h: (b, 0, h)),
                   pl.BlockSpec((1, S, LANES), lambda b, h: (b, 0, 0))],
        out_shape=[jax.ShapeDtypeStruct((B, S, heads * HEAD_DIM), F32),
                   jax.ShapeDtypeStruct((B, S, LANES), F32)],
        scratch_shapes=scratch,
        compiler_params=_params("arbitrary", "arbitrary"),
        name=f"dilated_attention_d{dil}",
    )(qk, qk, v)


def _mix_out_kernel(*refs, n_groups, heads):
    o_refs = refs[:n_groups]
    l_refs = refs[n_groups:2 * n_groups]
    g_ref, w_ref, x_ref, nw_ref, out_ref, ya_ref, yb_ref = refs[2 * n_groups:]
    s = pl.program_id(0)
    D = out_ref.shape[1]

    @pl.when(s == 0)
    def _():
        yb_ref[...] = jnp.zeros_like(yb_ref)

    def step(y_fill, y_use):
        lses = [r[...] for r in l_refs]
        mx = functools.reduce(jnp.maximum, lses)
        es = [jnp.exp(l - mx) for l in lses]
        inv = 1.0 / functools.reduce(lambda a, b: a + b, es)
        wts = [e * inv for e in es]
        for h in range(heads):
            sl = slice(h * HEAD_DIM, (h + 1) * HEAD_DIM)
            o = None
            for wt, o_ref in zip(wts, o_refs):
                t = wt[:, h:h + 1] * o_ref[:, sl]
                o = t if o is None else o + t
            g = g_ref[:, sl]
            y_fill[:, sl] = (o * _silu(g)).astype(y_fill.dtype)

        ssq = None
        for c in range(D // MXU_COLS):
            cols = slice(c * MXU_COLS, (c + 1) * MXU_COLS)
            x = x_ref[:, cols] + jnp.dot(y_use[...], w_ref[:, cols], preferred_element_type=F32)
            out_ref[:, cols] = x
            sq = x * x
            for hh in range(MXU_COLS // LANES):
                part = sq[:, hh * LANES:(hh + 1) * LANES]
                ssq = part if ssq is None else ssq + part
        scale = lax.rsqrt(jnp.sum(ssq, axis=-1, keepdims=True) * (1.0 / D) + RMS_EPS)
        out_ref[...] = out_ref[...] * scale * nw_ref[...]

    @pl.when(s % 2 == 0)
    def _():
        step(ya_ref, yb_ref)

    @pl.when(s % 2 == 1)
    def _():
        step(yb_ref, ya_ref)


def _mix_out(os_, lses, gate, w, x, norm_w, *, heads, tm):
    M, D = x.shape
    width = heads * HEAD_DIM
    n = len(os_)
    n_tiles = M // tm
    fill = lambda c: pl.BlockSpec((tm, c), lambda s: (jnp.minimum(s, n_tiles - 1), 0))
    use = lambda c: pl.BlockSpec((tm, c), lambda s: (jnp.maximum(s - 1, 0), 0))
    return pl.pallas_call(
        functools.partial(_mix_out_kernel, n_groups=n, heads=heads),
        grid=(n_tiles + 1,),
        in_specs=[fill(width)] * n + [fill(LANES)] * n + [
            fill(width),
            pl.BlockSpec((width, D), lambda s: (0, 0), pipeline_mode=pl.Buffered(1)),
            use(D),
            pl.BlockSpec((1, D), lambda s: (0, 0))],
        out_specs=use(D),
        out_shape=jax.ShapeDtypeStruct((M, D), F32),
        scratch_shapes=[pltpu.VMEM((tm, width), BF16), pltpu.VMEM((tm, width), BF16)],
        compiler_params=_params("arbitrary"),
        name="mix_out_norm",
    )(*os_, *lses, gate, w, x, norm_w.reshape(1, D))


def _rope_tables(S):
    half = ROPE_DIM // 2
    inv_freq = jnp.power(ROPE_THETA, -jnp.arange(half, dtype=F32) * (2.0 / ROPE_DIM))
    ang = jnp.arange(S).astype(F32)[:, None] * inv_freq[None, :]
    cos, sin = jnp.cos(ang), jnp.sin(ang)
    zeros = jnp.zeros((S, LANES - ROPE_DIM), F32)
    return (jnp.concatenate([cos, cos, 1.0 + zeros], axis=1),
            jnp.concatenate([-sin, sin, zeros], axis=1))


def _deinterleave(t, dil):
    S, C = t.shape
    return t.reshape(S // dil, dil, C).transpose(1, 0, 2).reshape(S, C)


def _forward(x, norm_w, final_norm_w, na_w_in, na_rpb, na_w_out, dl_w_in, dl_w_out, *,
             tm_norm=512, tm_norm1=256, tm=1024, tm_plain=1024, tn_qkv0=768, tn=512, tm_out=256):
    B, S, D = x.shape
    M = B * S
    assert norm_w.shape[0] == 2 and na_w_in.shape[0] == 1 and dl_w_in.shape[0] == 1
    na_width = na_w_out.shape[1]
    na_heads = na_width // HEAD_DIM
    dl_width = dl_w_out.shape[1]
    dl_heads = dl_width // HEAD_DIM
    n_groups = len(DL_GROUPS)

    (xn0,) = _rmsnorm(x, norm_w[0], (1,), tm_norm)
    xn0 = xn0.reshape(M, D)
    (qkv0,) = _proj(xn0, na_w_in[0], [_Seg(0, 3 * na_width, tn_qkv0, BF16, n_q_cols=na_width, rot=(GRID_W, NA_ROT))],
                    tm=tm_plain,
                    name="proj0_qkv")
    (gate0,) = _proj(xn0, na_w_in[0], [_Seg(3 * na_width, na_width, tn, F32)], tm=tm_plain, name="proj0_gate")
    y0 = _na_attention(qkv0.reshape(B, S, 3 * na_width), gate0.reshape(B, S, na_width), na_rpb[0],
                       heads=na_heads)
    (x1,) = _proj(y0.reshape(M, na_width), na_w_out[0], [_Seg(0, D, tn, F32, mode="res")], tm=tm_plain,
                  res=x.reshape(M, D), name="out0")

    w_in1 = dl_w_in[0]
    dils = tuple(sorted({d for _, d in DL_GROUPS}))
    xn1 = dict(zip(dils, _rmsnorm(x1.reshape(B, S, D), norm_w[1], dils, tm_norm1)))
    rope = _rope_tables(S)
    os_, lses = [], []
    for gi, (window, dil) in enumerate(DL_GROUPS):
        half = window // (2 * dil)
        off = gi * 3 * dl_width
        qk, v = _proj(xn1[dil].reshape(M, D), w_in1,
                      [_Seg(off, 2 * dl_width, tn, BF16, mode="rope", n_q_cols=dl_width),
                       _Seg(off + 2 * dl_width, dl_width, tn // 2, BF16)],
                      tm=tm, rope=[_deinterleave(t, dil) for t in rope], name=f"proj1_g{gi}")
        o, lse = _dilated_attention(qk.reshape(B, dil, S // dil, 2 * dl_width),
                                    v.reshape(B, dil, S // dil, dl_width), B=B, dil=dil, half=half,
                                    heads=dl_heads)
        os_.append(o.reshape(M, dl_width))
        lses.append(lse.reshape(M, LANES))
    (gate1,) = _proj(xn1[1].reshape(M, D), w_in1, [_Seg(3 * n_groups * dl_width, dl_width, tn, F32)], tm=tm_plain,
                     name="proj1_gate")
    out = _mix_out(os_, lses, gate1, dl_w_out[0].astype(BF16), x1, final_norm_w, heads=dl_heads, tm=tm_out)
    return out.reshape(B, S, D)


def kernel(x, norm_w, final_norm_w, na_w_in, na_rpb, na_w_out, dl_w_in, dl_w_out):
    return _forward(x, norm_w, final_norm_w, na_w_in, na_rpb, na_w_out, dl_w_in, dl_w_out)
```

```python
import functools
import math
from typing import Any, NamedTuple

import jax
import jax.numpy as jnp
from jax import lax
from jax.experimental import pallas as pl
from jax.experimental.pallas import tpu as pltpu

HEAD_DIM = 128
GRID_W = 64
NA_KH = 8
NA_KW = 16
DL_GROUPS = ((128, 1), (512, 4), (2048, 16))
ROPE_THETA = 500000.0
ROPE_DIM = HEAD_DIM // 4
RMS_EPS = 1e-6
NEG_INF = -1e30
SCALE = HEAD_DIM ** -0.5
LOG2E = math.log2(math.e)
LN2 = math.log(2.0)
Q_SCALE = SCALE * LOG2E

LANES = 128
SUBLANES = 8
MXU_COLS = 256
VMEM_LIMIT_BYTES = 56 * 1024 * 1024

BF16 = jnp.bfloat16
F32 = jnp.float32


def _params(*semantics):
    return pltpu.CompilerParams(dimension_semantics=semantics, vmem_limit_bytes=VMEM_LIMIT_BYTES)


def _silu(g):
    hg = 0.5 * g
    return hg + hg * jnp.tanh(hg)


def _attend(blocks):
    logits = [lax.dot_general(q, k, (((1,), (1,)), ((), ())), preferred_element_type=F32) + bias
              for q, k, _, bias in blocks]
    out = []
    for s, (_, _, v, _) in zip(logits, blocks):
        m = jnp.max(s, axis=-1, keepdims=True)
        p = jnp.exp2(s - m).astype(BF16)
        v_aug = jnp.concatenate([v, jnp.ones_like(v)], axis=1)
        pv = jnp.dot(p, v_aug, preferred_element_type=F32)
        den = pv[:, HEAD_DIM:]
        out.append((pv[:, :HEAD_DIM] / den, m, den))
    return out


def _rmsnorm_kernel(x_ref, g_ref, *rest, dils):
    out_refs = rest[:len(dils)]
    x = x_ref[0]
    y = x * lax.rsqrt(jnp.mean(x * x, axis=-1, keepdims=True) + RMS_EPS)
    y = y * g_ref[...]
    tm, D = x.shape
    n_chunks = D // LANES
    if any(d > 1 for d in dils):
        (y_ref,) = rest[len(dils):]
        for c in range(n_chunks):
            y_ref[c] = y[:, c * LANES:(c + 1) * LANES]
    for d, o_ref in zip(dils, out_refs):
        if d == 1:
            o_ref[0, 0] = y.astype(o_ref.dtype)
        else:
            for c in range(n_chunks):
                for r in range(d):
                    o_ref[0, r, :, c * LANES:(c + 1) * LANES] = (
                        y_ref[c, pl.ds(r, tm // d, stride=d), :].astype(o_ref.dtype))


def _rmsnorm(x, g, dils, tm):
    B, S, D = x.shape
    out_shape = [jax.ShapeDtypeStruct((B, d, S // d, D), BF16) for d in dils]
    out_specs = [pl.BlockSpec((1, d, tm // d, D), lambda b, i: (b, 0, i, 0)) for d in dils]
    return pl.pallas_call(
        functools.partial(_rmsnorm_kernel, dils=dils),
        grid=(B, S // tm),
        in_specs=[pl.BlockSpec((1, tm, D), lambda b, i: (b, i, 0)),
                  pl.BlockSpec((1, D), lambda b, i: (0, 0))],
        out_specs=out_specs,
        out_shape=out_shape,
        scratch_shapes=[pltpu.VMEM((D // LANES, tm, LANES), F32)] if any(d > 1 for d in dils) else [],
        compiler_params=_params("parallel", "parallel"),
        name="rmsnorm",
    )(x, g.reshape(1, D))


CAST_ROWS = 256


class _Seg(NamedTuple):
    col_off: int
    n_cols: int
    tn: int
    out_dtype: Any
    mode: str = "plain"
    n_q_cols: int = 0
    rot: Any = None


def _proj_kernel(a_ref, *refs, segs, has_res, has_rope):
    n = len(segs)
    w_refs, refs = refs[:n], refs[n:]
    if has_res:
        res_ref, refs = refs[0], refs[1:]
    if has_rope:
        (cos_ref, sin_ref), refs = refs[:2], refs[2:]
    o_refs, wb_refs = refs[:n], refs[n:]
    j = pl.program_id(0)

    @pl.when(pl.program_id(1) == 0)
    def _convert_weights():
        for w_ref, wb_ref in zip(w_refs, wb_refs):
            def body(r, carry, w_ref=w_ref, wb_ref=wb_ref):
                rows = pl.ds(pl.multiple_of(r * CAST_ROWS, CAST_ROWS), CAST_ROWS)
                wb_ref[rows, :] = w_ref[rows, :].astype(BF16)
                return carry
            lax.fori_loop(0, w_ref.shape[0] // CAST_ROWS, body, 0)

    half = ROPE_DIM // 2
    for seg, wb_ref, o_ref in zip(segs, wb_refs, o_refs):
        for c in range(seg.tn // MXU_COLS):
            cols = slice(c * MXU_COLS, (c + 1) * MXU_COLS)
            hm = a_ref.shape[0] // 2
            acc = jnp.concatenate(
                [jnp.dot(a_ref[0:hm, :], wb_ref[:, cols], preferred_element_type=F32),
                 jnp.dot(a_ref[hm:, :], wb_ref[:, cols], preferred_element_type=F32)], axis=0)
            if seg.mode == "res":
                acc = res_ref[:, cols] + acc
            elif seg.mode == "rope":
                fac = jnp.where(j * seg.tn < seg.n_q_cols, Q_SCALE, 1.0).astype(F32)
                first_half = lax.broadcasted_iota(jnp.int32, (acc.shape[0], LANES), 1) < half
                heads = []
                for hh in range(MXU_COLS // LANES):
                    x = acc[:, hh * LANES:(hh + 1) * LANES] * fac
                    partner = jnp.where(first_half, pltpu.roll(x, LANES - half, 1), pltpu.roll(x, half, 1))
                    heads.append(x * cos_ref[...] + partner * sin_ref[...])
                acc = jnp.concatenate(heads, axis=1)
            elif seg.n_q_cols:
                is_q = j * seg.tn + c * MXU_COLS < seg.n_q_cols
                if seg.rot:
                    acc = jnp.where(is_q, acc * Q_SCALE, _rot_rows(acc, *seg.rot))
                else:
                    acc = acc * jnp.where(is_q, Q_SCALE, 1.0).astype(F32)
            o_ref[:, cols] = acc.astype(o_ref.dtype)


def _proj(a, w, segs, *, tm, res=None, rope=None, name):
    M, K = a.shape
    steps = segs[0].n_cols // segs[0].tn
    assert M % tm == 0 and K % CAST_ROWS == 0
    in_specs = [pl.BlockSpec((tm, K), lambda j, i: (i, 0))]
    args = [a]
    for seg in segs:
        assert seg.n_cols == steps * seg.tn and seg.col_off % seg.tn == 0 and seg.tn % MXU_COLS == 0
        in_specs.append(pl.BlockSpec((K, seg.tn), lambda j, i, off=seg.col_off // seg.tn: (0, j + off)))
        args.append(w)
    if res is not None:
        assert len(segs) == 1 and segs[0].mode == "res"
        in_specs.append(pl.BlockSpec((tm, segs[0].tn), lambda j, i: (i, j)))
        args.append(res)
    if rope is not None:
        (seg,) = [s for s in segs if s.mode == "rope"]
        P = rope[0].shape[0]
        assert P % tm == 0 and seg.n_q_cols % seg.tn == 0
        nper = P // tm
        for t in rope:
            in_specs.append(pl.BlockSpec((tm, LANES), lambda j, i: (i % nper, 0)))
            args.append(t)
    return pl.pallas_call(
        functools.partial(_proj_kernel, segs=tuple(segs), has_res=res is not None, has_rope=rope is not None),
        grid=(steps, M // tm),
        in_specs=in_specs,
        out_specs=[pl.BlockSpec((tm, seg.tn), lambda j, i: (i, j)) for seg in segs],
        out_shape=[jax.ShapeDtypeStruct((M, seg.n_cols), seg.out_dtype) for seg in segs],
        scratch_shapes=[pltpu.VMEM((K, seg.tn), BF16) for seg in segs],
        compiler_params=_params("arbitrary", "arbitrary"),
        name=name,
    )(*args)


NA_QROWS = 8
NA_KROWS = 16
NA_QCOLS = 16
NA_KCOLS = 32
NA_ROT = NA_KW // 2
NA_RB = 16


def _rot_rows(x, group, shift):
    pieces = []
    for g in range(x.shape[0] // group):
        blk = x[g * group:(g + 1) * group]
        pieces += [blk[group - shift:], blk[:group - shift]]
    return jnp.concatenate(pieces, axis=0)


def _na_block_tables(rows):
    nblk = rows // NA_QROWS
    kb0s, vids, variants = [], [], []
    for i in range(nblk):
        kb0 = min(max(NA_QROWS * i - NA_KH // 2, 0), rows - NA_KROWS)
        tab = []
        for a in range(NA_QROWS):
            r = NA_QROWS * i + a
            rs = min(max(r - NA_KH // 2, 0), rows - NA_KH)
            for kl in range(NA_KROWS):
                kr = kb0 + kl
                tab.append(kr - r + NA_KH - 1 if rs <= kr < rs + NA_KH else None)
        tab = tuple(tab)
        if tab not in variants:
            variants.append(tab)
        kb0s.append(kb0)
        vids.append(variants.index(tab))
    return kb0s, vids, variants


def _na_kernel(rpb_ref, kb0_ref, vid_ref, q_ref, k_ref, v_ref, g_ref, o_ref, tile_ref, bias_ref,
               *, variants, nblk):
    h = pl.program_id(0)
    W = GRID_W
    n_ct = W // NA_QCOLS
    n_dr, n_dc = 2 * NA_KH - 1, 2 * NA_KW - 1
    per = LANES // NA_KCOLS

    @pl.when(pl.program_id(1) == 0)
    def _build_bias():
        qcl = lax.broadcasted_iota(jnp.int32, (NA_QCOLS, LANES), 0)
        lane = lax.broadcasted_iota(jnp.int32, (NA_QCOLS, LANES), 1)
        kcl = lane & (NA_KCOLS - 1)
        for ct in range(n_ct):
            qc = ct * NA_QCOLS + qcl
            kc = (ct * NA_QCOLS + kcl - NA_ROT) & (W - 1)
            cs = jnp.clip(qc - NA_KW // 2, 0, W - NA_KW)
            col_ok = (kc >= cs) & (kc < cs + NA_KW)
            dc = jnp.clip(kc - qc + NA_KW - 1, 0, n_dc - 1)
            for dr in range(n_dr):
                table = jnp.broadcast_to(rpb_ref[0, dr:dr + 1, :] * LOG2E, (NA_QCOLS, LANES))
                tile_ref[ct, dr] = jnp.where(col_ok, jnp.take_along_axis(table, dc, axis=1), NEG_INF)
        neg = jnp.full((NA_QCOLS, LANES), NEG_INF, F32)
        part = lane // NA_KCOLS
        for vi, tab in enumerate(variants):
            for ct in range(n_ct):
                for a in range(NA_QROWS):
                    for m in range(NA_KROWS // per):
                        piece = neg
                        for i in range(per):
                            d = tab[a * NA_KROWS + per * m + i]
                            if d is not None:
                                piece = jnp.where(part == i, tile_ref[ct, d], piece)
                        bias_ref[vi, ct, a * NA_QCOLS:(a + 1) * NA_QCOLS, m * LANES:(m + 1) * LANES] = piece

    def rows_of(ref, row0, n_rows, cols):
        band = ref.at[0, pl.ds(pl.multiple_of(row0 * W, W), n_rows * W), :]
        return jnp.concatenate(
            [band[a * W + c:a * W + c + NA_QCOLS, :] for a in range(n_rows) for c in cols], axis=0)

    def load(i, ct):
        kpos = [(ct * NA_QCOLS + u * NA_QCOLS) % W for u in range(NA_KCOLS // NA_QCOLS)]
        return (rows_of(q_ref, i * NA_QROWS, NA_QROWS, [ct * NA_QCOLS]),
                rows_of(k_ref, kb0_ref[i], NA_KROWS, kpos),
                rows_of(v_ref, kb0_ref[i], NA_KROWS, kpos),
                bias_ref[vid_ref[i], ct])

    n_rb = math.gcd(nblk, NA_RB)

    def group(n, carry):
        tiles = [(n * n_rb + rb, ct) for rb in range(n_rb) for ct in range(n_ct)]
        for (i, ct), (o, _, _) in zip(tiles, _attend([load(i, ct) for i, ct in tiles])):
            g = rows_of(g_ref, i * NA_QROWS, NA_QROWS, [ct * NA_QCOLS])
            y = (o * _silu(g)).astype(o_ref.dtype)
            band = o_ref.at[0, pl.ds(pl.multiple_of(i * NA_QROWS * W, W), NA_QROWS * W), :]
            for a in range(NA_QROWS):
                band[a * W + ct * NA_QCOLS:a * W + (ct + 1) * NA_QCOLS, :] = y[a * NA_QCOLS:(a + 1) * NA_QCOLS, :]
        return carry

    lax.fori_loop(0, nblk // n_rb, group, 0)


def _na_attention(qkv, gate, rpb, *, heads):
    B, S, _ = qkv.shape
    rows = S // GRID_W
    assert rows % NA_QROWS == 0 and rows >= NA_KROWS
    kb0s, vids, variants = _na_block_tables(rows)
    nblk = rows // NA_QROWS
    n_ct = GRID_W // NA_QCOLS
    smem = pl.BlockSpec(memory_space=pltpu.SMEM)
    tok = lambda off: pl.BlockSpec((1, S, HEAD_DIM), lambda h, b: (b, 0, h + off))
    n_dr, n_dc = rpb.shape[1:]
    rpb_rows = -(-n_dr // SUBLANES) * SUBLANES
    rpb = jnp.pad(rpb, ((0, 0), (0, rpb_rows - n_dr), (0, LANES - n_dc)))
    return pl.pallas_call(
        functools.partial(_na_kernel, variants=variants, nblk=nblk),
        grid=(heads, B),
        in_specs=[pl.BlockSpec((1, rpb_rows, LANES), lambda h, b: (h, 0, 0)), smem, smem,
                  tok(0), tok(heads), tok(2 * heads), tok(0)],
        out_specs=tok(0),
        out_shape=jax.ShapeDtypeStruct((B, S, heads * HEAD_DIM), BF16),
        scratch_shapes=[pltpu.VMEM((n_ct, 2 * NA_KH - 1, NA_QCOLS, LANES), F32),
                        pltpu.VMEM((len(variants), n_ct, NA_QROWS * NA_QCOLS, NA_KROWS * NA_KCOLS), F32)],
        compiler_params=_params("arbitrary", "arbitrary"),
        name="na_attention",
    )(rpb, jnp.asarray(kb0s, jnp.int32), jnp.asarray(vids, jnp.int32), qkv, qkv, qkv, gate)


DL_QB = 128
DL_UNROLL = 32
DL_MAX_STRIDE = 4


def _dilated_kernel(q_ref, k_ref, v_ref, o_ref, lse_ref, mask_ref, *stage_refs, dil, half, L):
    h = pl.program_id(1)
    qb = min(DL_QB, L - 2 * half)
    kb = qb + 2 * half
    nqb = L // qb
    d_in = min(dil, DL_MAX_STRIDE)
    d_out = dil // d_in

    @pl.when(h == 0)
    def _():
        lse_ref[...] = jnp.zeros_like(lse_ref)

    @pl.when((pl.program_id(0) == 0) & (h == 0))
    def _():
        rel = (lax.broadcasted_iota(jnp.int32, (qb, kb), 1) - lax.broadcasted_iota(jnp.int32, (qb, kb), 0))
        for vi in range(3):
            mask_ref[vi] = jnp.where(jnp.abs(rel - vi * half) <= half, 0.0, NEG_INF)

    def load(r, q0):
        k0 = pl.multiple_of(jnp.clip(q0 - half, 0, L - kb), half)
        return (q_ref[0, r, pl.ds(q0, qb), :], k_ref[0, r, pl.ds(k0, kb), :], v_ref[0, r, pl.ds(k0, kb), :],
                mask_ref[(q0 - k0) // half])

    def put_lse(rows, lse):
        lane = lax.broadcasted_iota(jnp.int32, lse.shape, 1)
        lse_ref[0, rows, :] = jnp.where(lane == h, lse, lse_ref[0, rows, :])

    if d_out == 1:
        unroll = math.gcd(dil * nqb, DL_UNROLL)

        def group(t, carry):
            blocks = [((t * unroll + u) // nqb, pl.multiple_of(((t * unroll + u) % nqb) * qb, qb))
                      for u in range(unroll)]
            for (r, q0), (o, m, den) in zip(blocks, _attend([load(r, q0) for r, q0 in blocks])):
                rows = pl.ds(q0, qb) if dil == 1 else pl.ds(q0 * dil + r, qb, stride=dil)
                o_ref[0, rows, :] = o
                put_lse(rows, m * LN2 + jnp.log(den))
            return carry

        lax.fori_loop(0, dil * nqb // unroll, group, 0)
    else:
        o_tmp, l_tmp = stage_refs

        def group(r0, carry):
            blocks = [(qi, r1) for qi in range(nqb) for r1 in range(d_out)]
            outs = _attend([load(r1 * d_in + r0, qi * qb) for qi, r1 in blocks])
            for (qi, r1), (o, m, den) in zip(blocks, outs):
                hop1 = pl.ds(r1, qb, stride=d_out)
                o_tmp[qi, hop1, :] = o
                l_tmp[qi, hop1, :] = m * LN2 + jnp.log(den)
            for qi in range(nqb):
                hop2 = pl.ds(qi * qb * dil + r0, qb * d_out, stride=d_in)
                o_ref[0, hop2, :] = o_tmp[qi]
                put_lse(hop2, l_tmp[qi])
            return carry

        lax.fori_loop(0, d_in, group, 0)


def _dilated_attention(qk, v, *, B, dil, half, heads):
    L = qk.shape[2]
    S = L * dil
    qb = min(DL_QB, L - 2 * half)
    assert L % qb == 0 and heads <= LANES
    d_out = dil // min(dil, DL_MAX_STRIDE)
    assert dil % min(dil, DL_MAX_STRIDE) == 0
    scratch = [pltpu.VMEM((3, qb, qb + 2 * half), F32)]
    if d_out > 1:
        scratch += [pltpu.VMEM((L // qb, qb * d_out, LANES), F32)] * 2
    blk = lambda off: pl.BlockSpec((1, dil, L, HEAD_DIM), lambda b, h: (b, 0, 0, h + off))
    return pl.pallas_call(
        functools.partial(_dilated_kernel, dil=dil, half=half, L=L),
        grid=(B, heads),
        in_specs=[blk(0), blk(heads), blk(0)],
        out_specs=[pl.BlockSpec((1, S, HEAD_DIM), lambda b, h: (b, 0, h)),
                   pl.BlockSpec((1, S, LANES), lambda b, h: (b, 0, 0))],
        out_shape=[jax.ShapeDtypeStruct((B, S, heads * HEAD_DIM), F32),
                   jax.ShapeDtypeStruct((B, S, LANES), F32)],
        scratch_shapes=scratch,
        compiler_params=_params("arbitrary", "arbitrary"),
        name=f"dilated_attention_d{dil}",
    )(qk, qk, v)


def _mix_out_kernel(*refs, n_groups, heads):
    o_refs = refs[:n_groups]
    l_refs = refs[n_groups:2 * n_groups]
    g_ref, w_ref, x_ref, nw_ref, out_ref, ya_ref, yb_ref = refs[2 * n_groups:]
    s = pl.program_id(0)
    D = out_ref.shape[1]

    @pl.when(s == 0)
    def _():
        yb_ref[...] = jnp.zeros_like(yb_ref)

    def step(y_fill, y_use):
        lses = [r[...] for r in l_refs]
        mx = functools.reduce(jnp.maximum, lses)
        es = [jnp.exp(l - mx) for l in lses]
        inv = 1.0 / functools.reduce(lambda a, b: a + b, es)
        wts = [e * inv for e in es]
        for h in range(heads):
            sl = slice(h * HEAD_DIM, (h + 1) * HEAD_DIM)
            o = None
            for wt, o_ref in zip(wts, o_refs):
                t = wt[:, h:h + 1] * o_ref[:, sl]
                o = t if o is None else o + t
            g = g_ref[:, sl]
            y_fill[:, sl] = (o * _silu(g)).astype(y_fill.dtype)

        ssq = None
        for c in range(D // MXU_COLS):
            cols = slice(c * MXU_COLS, (c + 1) * MXU_COLS)
            x = x_ref[:, cols] + jnp.dot(y_use[...], w_ref[:, cols], preferred_element_type=F32)
            out_ref[:, cols] = x
            sq = x * x
            for hh in range(MXU_COLS // LANES):
                part = sq[:, hh * LANES:(hh + 1) * LANES]
                ssq = part if ssq is None else ssq + part
        scale = lax.rsqrt(jnp.sum(ssq, axis=-1, keepdims=True) * (1.0 / D) + RMS_EPS)
        out_ref[...] = out_ref[...] * scale * nw_ref[...]

    @pl.when(s % 2 == 0)
    def _():
        step(ya_ref, yb_ref)

    @pl.when(s % 2 == 1)
    def _():
        step(yb_ref, ya_ref)


def _mix_out(os_, lses, gate, w, x, norm_w, *, heads, tm):
    M, D = x.shape
    width = heads * HEAD_DIM
    n = len(os_)
    n_tiles = M // tm
    fill = lambda c: pl.BlockSpec((tm, c), lambda s: (jnp.minimum(s, n_tiles - 1), 0))
    use = lambda c: pl.BlockSpec((tm, c), lambda s: (jnp.maximum(s - 1, 0), 0))
    return pl.pallas_call(
        functools.partial(_mix_out_kernel, n_groups=n, heads=heads),
        grid=(n_tiles + 1,),
        in_specs=[fill(width)] * n + [fill(LANES)] * n + [
            fill(width),
            pl.BlockSpec((width, D), lambda s: (0, 0), pipeline_mode=pl.Buffered(1)),
            use(D),
            pl.BlockSpec((1, D), lambda s: (0, 0))],
        out_specs=use(D),
        out_shape=jax.ShapeDtypeStruct((M, D), F32),
        scratch_shapes=[pltpu.VMEM((tm, width), BF16), pltpu.VMEM((tm, width), BF16)],
        compiler_params=_params("arbitrary"),
        name="mix_out_norm",
    )(*os_, *lses, gate, w, x, norm_w.reshape(1, D))


def _rope_tables(S):
    half = ROPE_DIM // 2
    inv_freq = jnp.power(ROPE_THETA, -jnp.arange(half, dtype=F32) * (2.0 / ROPE_DIM))
    ang = jnp.arange(S).astype(F32)[:, None] * inv_freq[None, :]
    cos, sin = jnp.cos(ang), jnp.sin(ang)
    zeros = jnp.zeros((S, LANES - ROPE_DIM), F32)
    return (jnp.concatenate([cos, cos, 1.0 + zeros], axis=1),
            jnp.concatenate([-sin, sin, zeros], axis=1))


def _deinterleave(t, dil):
    S, C = t.shape
    return t.reshape(S // dil, dil, C).transpose(1, 0, 2).reshape(S, C)


def _forward(x, norm_w, final_norm_w, na_w_in, na_rpb, na_w_out, dl_w_in, dl_w_out, *,
             tm_norm=512, tm_norm1=256, tm=1024, tm_plain=1024, tn_qkv0=768, tn=512, tm_out=256):
    B, S, D = x.shape
    M = B * S
    assert norm_w.shape[0] == 2 and na_w_in.shape[0] == 1 and dl_w_in.shape[0] == 1
    na_width = na_w_out.shape[1]
    na_heads = na_width // HEAD_DIM
    dl_width = dl_w_out.shape[1]
    dl_heads = dl_width // HEAD_DIM
    n_groups = len(DL_GROUPS)

    (xn0,) = _rmsnorm(x, norm_w[0], (1,), tm_norm)
    xn0 = xn0.reshape(M, D)
    (qkv0,) = _proj(xn0, na_w_in[0], [_Seg(0, 3 * na_width, tn_qkv0, BF16, n_q_cols=na_width, rot=(GRID_W, NA_ROT))],
                    tm=tm_plain,
                    name="proj0_qkv")
    (gate0,) = _proj(xn0, na_w_in[0], [_Seg(3 * na_width, na_width, tn, F32)], tm=tm_plain, name="proj0_gate")
    y0 = _na_attention(qkv0.reshape(B, S, 3 * na_width), gate0.reshape(B, S, na_width), na_rpb[0],
                       heads=na_heads)
    (x1,) = _proj(y0.reshape(M, na_width), na_w_out[0], [_Seg(0, D, tn, F32, mode="res")], tm=tm_plain,
                  res=x.reshape(M, D), name="out0")

    w_in1 = dl_w_in[0]
    dils = tuple(sorted({d for _, d in DL_GROUPS}))
    xn1 = dict(zip(dils, _rmsnorm(x1.reshape(B, S, D), norm_w[1], dils, tm_norm1)))
    rope = _rope_tables(S)
    os_, lses = [], []
    for gi, (window, dil) in enumerate(DL_GROUPS):
        half = window // (2 * dil)
        off = gi * 3 * dl_width
        qk, v = _proj(xn1[dil].reshape(M, D), w_in1,
                      [_Seg(off, 2 * dl_width, tn, BF16, mode="rope", n_q_cols=dl_width),
                       _Seg(off + 2 * dl_width, dl_width, tn // 2, BF16)],
                      tm=tm, rope=[_deinterleave(t, dil) for t in rope], name=f"proj1_g{gi}")
        o, lse = _dilated_attention(qk.reshape(B, dil, S // dil, 2 * dl_width),
                                    v.reshape(B, dil, S // dil, dl_width), B=B, dil=dil, half=half,
                                    heads=dl_heads)
        os_.append(o.reshape(M, dl_width))
        lses.append(lse.reshape(M, LANES))
    (gate1,) = _proj(xn1[1].reshape(M, D), w_in1, [_Seg(3 * n_groups * dl_width, dl_width, tn, F32)], tm=tm_plain,
                     name="proj1_gate")
    out = _mix_out(os_, lses, gate1, dl_w_out[0].astype(BF16), x1, final_norm_w, heads=dl_heads, tm=tm_out)
    return out.reshape(B, S, D)


def kernel(x, norm_w, final_norm_w, na_w_in, na_rpb, na_w_out, dl_w_in, dl_w_out):
    return _forward(x, norm_w, final_norm_w, na_w_in, na_rpb, na_w_out, dl_w_in, dl_w_out)
```

```python
import functools
import math
from typing import Any, NamedTuple

import jax
import jax.numpy as jnp
from jax import lax
from jax.experimental import pallas as pl
from jax.experimental.pallas import tpu as pltpu

HEAD_DIM = 128
GRID_W = 64
NA_KH = 8
NA_KW = 16
DL_GROUPS = ((128, 1), (512, 4), (2048, 16))
ROPE_THETA = 500000.0
ROPE_DIM = HEAD_DIM // 4
RMS_EPS = 1e-6
NEG_INF = -1e30
SCALE = HEAD_DIM ** -0.5
LOG2E = math.log2(math.e)
LN2 = math.log(2.0)
Q_SCALE = SCALE * LOG2E

LANES = 128
SUBLANES = 8
MXU_COLS = 256
MAX_ROW_STRIDE = 4
VMEM_LIMIT_BYTES = 56 * 1024 * 1024

BF16 = jnp.bfloat16
F32 = jnp.float32


def _params(*semantics):
    return pltpu.CompilerParams(dimension_semantics=semantics, vmem_limit_bytes=VMEM_LIMIT_BYTES)


def _silu(g):
    hg = 0.5 * g
    return hg + hg * jnp.tanh(hg)


def _attend(blocks):
    logits = [lax.dot_general(q, k, (((1,), (1,)), ((), ())), preferred_element_type=F32) + bias
              for q, k, _, bias in blocks]
    out = []
    for s, (_, _, v, _) in zip(logits, blocks):
        m = jnp.max(s, axis=-1, keepdims=True)
        p = jnp.exp2(s - m).astype(BF16)
        v_aug = jnp.concatenate([v, jnp.ones_like(v)], axis=1)
        pv = jnp.dot(p, v_aug, preferred_element_type=F32)
        den = pv[:, HEAD_DIM:]
        out.append((pv[:, :HEAD_DIM] / den, m, den))
    return out


def _rmsnorm_kernel(x_ref, g_ref, *rest, dils):
    out_refs, scratch = rest[:len(dils)], rest[len(dils):]
    x = x_ref[0]
    y = x * lax.rsqrt(jnp.mean(x * x, axis=-1, keepdims=True) + RMS_EPS)
    y = y * g_ref[...]
    tm, D = x.shape
    n_chunks = D // LANES
    if scratch:
        y_ref, z_ref = scratch
        for c in range(n_chunks):
            y_ref[c] = y[:, c * LANES:(c + 1) * LANES]
    for d, o_ref in zip(dils, out_refs):
        if d == 1:
            o_ref[0, 0] = y.astype(o_ref.dtype)
            continue
        d_in = min(d, MAX_ROW_STRIDE)
        d_out = d // d_in
        for c in range(n_chunks):
            lanes = slice(c * LANES, (c + 1) * LANES)
            for r0 in range(d_in):
                z = y_ref[c, pl.ds(r0, tm // d_in, stride=d_in), :]
                if d_out == 1:
                    o_ref[0, r0, :, lanes] = z.astype(o_ref.dtype)
                    continue
                z_ref[c, r0] = z
                for r1 in range(d_out):
                    o_ref[0, r1 * d_in + r0, :, lanes] = (
                        z_ref[c, r0, pl.ds(r1, tm // d, stride=d_out), :].astype(o_ref.dtype))


def _rmsnorm(x, g, dils, tm):
    B, S, D = x.shape
    out_shape = [jax.ShapeDtypeStruct((B, d, S // d, D), BF16) for d in dils]
    out_specs = [pl.BlockSpec((1, d, tm // d, D), lambda b, i: (b, 0, i, 0)) for d in dils]
    return pl.pallas_call(
        functools.partial(_rmsnorm_kernel, dils=dils),
        grid=(B, S // tm),
        in_specs=[pl.BlockSpec((1, tm, D), lambda b, i: (b, i, 0)),
                  pl.BlockSpec((1, D), lambda b, i: (0, 0))],
        out_specs=out_specs,
        out_shape=out_shape,
        scratch_shapes=([pltpu.VMEM((D // LANES, tm, LANES), F32),
                         pltpu.VMEM((D // LANES, MAX_ROW_STRIDE, tm // MAX_ROW_STRIDE, LANES), F32)]
                        if any(d > 1 for d in dils) else []),
        compiler_params=_params("parallel", "parallel"),
        name="rmsnorm",
    )(x, g.reshape(1, D))


CAST_ROWS = 256


class _Seg(NamedTuple):
    col_off: int
    n_cols: int
    tn: int
    out_dtype: Any
    mode: str = "plain"
    n_q_cols: int = 0
    rot: Any = None


def _proj_kernel(a_ref, *refs, segs, has_res, has_rope):
    n = len(segs)
    w_refs, refs = refs[:n], refs[n:]
    if has_res:
        res_ref, refs = refs[0], refs[1:]
    if has_rope:
        (cos_ref, sin_ref), refs = refs[:2], refs[2:]
    o_refs, wb_refs = refs[:n], refs[n:]
    j = pl.program_id(0)

    @pl.when(pl.program_id(1) == 0)
    def _convert_weights():
        for w_ref, wb_ref in zip(w_refs, wb_refs):
            def body(r, carry, w_ref=w_ref, wb_ref=wb_ref):
                rows = pl.ds(pl.multiple_of(r * CAST_ROWS, CAST_ROWS), CAST_ROWS)
                wb_ref[rows, :] = w_ref[rows, :].astype(BF16)
                return carry
            lax.fori_loop(0, w_ref.shape[0] // CAST_ROWS, body, 0)

    half = ROPE_DIM // 2
    for seg, wb_ref, o_ref in zip(segs, wb_refs, o_refs):
        for c in range(seg.tn // MXU_COLS):
            cols = slice(c * MXU_COLS, (c + 1) * MXU_COLS)
            hm = a_ref.shape[0] // 2
            acc = jnp.concatenate(
                [jnp.dot(a_ref[0:hm, :], wb_ref[:, cols], preferred_element_type=F32),
                 jnp.dot(a_ref[hm:, :], wb_ref[:, cols], preferred_element_type=F32)], axis=0)
            if seg.mode == "res":
                acc = res_ref[:, cols] + acc
            elif seg.mode == "rope":
                fac = jnp.where(j * seg.tn < seg.n_q_cols, Q_SCALE, 1.0).astype(F32)
                first_half = lax.broadcasted_iota(jnp.int32, (acc.shape[0], LANES), 1) < half
                heads = []
                for hh in range(MXU_COLS // LANES):
                    x = acc[:, hh * LANES:(hh + 1) * LANES] * fac
                    partner = jnp.where(first_half, pltpu.roll(x, LANES - half, 1), pltpu.roll(x, half, 1))
                    heads.append(x * cos_ref[...] + partner * sin_ref[...])
                acc = jnp.concatenate(heads, axis=1)
            elif seg.n_q_cols:
                is_q = j * seg.tn + c * MXU_COLS < seg.n_q_cols
                if seg.rot:
                    acc = jnp.where(is_q, acc * Q_SCALE, _rot_rows(acc, *seg.rot))
                else:
                    acc = acc * jnp.where(is_q, Q_SCALE, 1.0).astype(F32)
            o_ref[:, cols] = acc.astype(o_ref.dtype)


def _proj(a, w, segs, *, tm, res=None, rope=None, name):
    M, K = a.shape
    steps = segs[0].n_cols // segs[0].tn
    assert M % tm == 0 and K % CAST_ROWS == 0
    in_specs = [pl.BlockSpec((tm, K), lambda j, i: (i, 0))]
    args = [a]
    for seg in segs:
        assert seg.n_cols == steps * seg.tn and seg.col_off % seg.tn == 0 and seg.tn % MXU_COLS == 0
        in_specs.append(pl.BlockSpec((K, seg.tn), lambda j, i, off=seg.col_off // seg.tn: (0, j + off)))
        args.append(w)
    if res is not None:
        assert len(segs) == 1 and segs[0].mode == "res"
        in_specs.append(pl.BlockSpec((tm, segs[0].tn), lambda j, i: (i, j)))
        args.append(res)
    if rope is not None:
        (seg,) = [s for s in segs if s.mode == "rope"]
        P = rope[0].shape[0]
        assert P % tm == 0 and seg.n_q_cols % seg.tn == 0
        nper = P // tm
        for t in rope:
            in_specs.append(pl.BlockSpec((tm, LANES), lambda j, i: (i % nper, 0)))
            args.append(t)
    return pl.pallas_call(
        functools.partial(_proj_kernel, segs=tuple(segs), has_res=res is not None, has_rope=rope is not None),
        grid=(steps, M // tm),
        in_specs=in_specs,
        out_specs=[pl.BlockSpec((tm, seg.tn), lambda j, i: (i, j)) for seg in segs],
        out_shape=[jax.ShapeDtypeStruct((M, seg.n_cols), seg.out_dtype) for seg in segs],
        scratch_shapes=[pltpu.VMEM((K, seg.tn), BF16) for seg in segs],
        compiler_params=_params("arbitrary", "arbitrary"),
        name=name,
    )(*args)


NA_QROWS = 8
NA_KROWS = 16
NA_QCOLS = 16
NA_KCOLS = 32
NA_ROT = NA_KW // 2
NA_RB = 16


def _rot_rows(x, group, shift):
    pieces = []
    for g in range(x.shape[0] // group):
        blk = x[g * group:(g + 1) * group]
        pieces += [blk[group - shift:], blk[:group - shift]]
    return jnp.concatenate(pieces, axis=0)


def _na_block_tables(rows):
    nblk = rows // NA_QROWS
    kb0s, vids, variants = [], [], []
    for i in range(nblk):
        kb0 = min(max(NA_QROWS * i - NA_KH // 2, 0), rows - NA_KROWS)
        tab = []
        for a in range(NA_QROWS):
            r = NA_QROWS * i + a
            rs = min(max(r - NA_KH // 2, 0), rows - NA_KH)
            for kl in range(NA_KROWS):
                kr = kb0 + kl
                tab.append(kr - r + NA_KH - 1 if rs <= kr < rs + NA_KH else None)
        tab = tuple(tab)
        if tab not in variants:
            variants.append(tab)
        kb0s.append(kb0)
        vids.append(variants.index(tab))
    return kb0s, vids, variants


def _na_kernel(rpb_ref, kb0_ref, vid_ref, q_ref, k_ref, v_ref, g_ref, o_ref, tile_ref, bias_ref,
               *, variants, nblk):
    h = pl.program_id(0)
    W = GRID_W
    n_ct = W // NA_QCOLS
    n_dr, n_dc = 2 * NA_KH - 1, 2 * NA_KW - 1
    per = LANES // NA_KCOLS

    @pl.when(pl.program_id(1) == 0)
    def _build_bias():
        qcl = lax.broadcasted_iota(jnp.int32, (NA_QCOLS, LANES), 0)
        lane = lax.broadcasted_iota(jnp.int32, (NA_QCOLS, LANES), 1)
        kcl = lane & (NA_KCOLS - 1)
        for ct in range(n_ct):
            qc = ct * NA_QCOLS + qcl
            kc = (ct * NA_QCOLS + kcl - NA_ROT) & (W - 1)
            cs = jnp.clip(qc - NA_KW // 2, 0, W - NA_KW)
            col_ok = (kc >= cs) & (kc < cs + NA_KW)
            dc = jnp.clip(kc - qc + NA_KW - 1, 0, n_dc - 1)
            for dr in range(n_dr):
                table = jnp.broadcast_to(rpb_ref[0, dr:dr + 1, :] * LOG2E, (NA_QCOLS, LANES))
                tile_ref[ct, dr] = jnp.where(col_ok, jnp.take_along_axis(table, dc, axis=1), NEG_INF)
        neg = jnp.full((NA_QCOLS, LANES), NEG_INF, F32)
        part = lane // NA_KCOLS
        for vi, tab in enumerate(variants):
            for ct in range(n_ct):
                for a in range(NA_QROWS):
                    for m in range(NA_KROWS // per):
                        piece = neg
                        for i in range(per):
                            d = tab[a * NA_KROWS + per * m + i]
                            if d is not None:
                                piece = jnp.where(part == i, tile_ref[ct, d], piece)
                        bias_ref[vi, ct, a * NA_QCOLS:(a + 1) * NA_QCOLS, m * LANES:(m + 1) * LANES] = piece

    def rows_of(ref, row0, n_rows, cols):
        band = ref.at[0, pl.ds(pl.multiple_of(row0 * W, W), n_rows * W), :]
        return jnp.concatenate(
            [band[a * W + c:a * W + c + NA_QCOLS, :] for a in range(n_rows) for c in cols], axis=0)

    def load(i, ct):
        kpos = [(ct * NA_QCOLS + u * NA_QCOLS) % W for u in range(NA_KCOLS // NA_QCOLS)]
        return (rows_of(q_ref, i * NA_QROWS, NA_QROWS, [ct * NA_QCOLS]),
                rows_of(k_ref, kb0_ref[i], NA_KROWS, kpos),
                rows_of(v_ref, kb0_ref[i], NA_KROWS, kpos),
                bias_ref[vid_ref[i], ct])

    n_rb = math.gcd(nblk, NA_RB)

    def group(n, carry):
        tiles = [(n * n_rb + rb, ct) for rb in range(n_rb) for ct in range(n_ct)]
        for (i, ct), (o, _, _) in zip(tiles, _attend([load(i, ct) for i, ct in tiles])):
            g = rows_of(g_ref, i * NA_QROWS, NA_QROWS, [ct * NA_QCOLS])
            y = (o * _silu(g)).astype(o_ref.dtype)
            band = o_ref.at[0, pl.ds(pl.multiple_of(i * NA_QROWS * W, W), NA_QROWS * W), :]
            for a in range(NA_QROWS):
                band[a * W + ct * NA_QCOLS:a * W + (ct + 1) * NA_QCOLS, :] = y[a * NA_QCOLS:(a + 1) * NA_QCOLS, :]
        return carry

    lax.fori_loop(0, nblk // n_rb, group, 0)


def _na_attention(qkv, gate, rpb, *, heads):
    B, S, _ = qkv.shape
    rows = S // GRID_W
    assert rows % NA_QROWS == 0 and rows >= NA_KROWS
    kb0s, vids, variants = _na_block_tables(rows)
    nblk = rows // NA_QROWS
    n_ct = GRID_W // NA_QCOLS
    smem = pl.BlockSpec(memory_space=pltpu.SMEM)
    tok = lambda off: pl.BlockSpec((1, S, HEAD_DIM), lambda h, b: (b, 0, h + off))
    n_dr, n_dc = rpb.shape[1:]
    rpb_rows = -(-n_dr // SUBLANES) * SUBLANES
    rpb = jnp.pad(rpb, ((0, 0), (0, rpb_rows - n_dr), (0, LANES - n_dc)))
    return pl.pallas_call(
        functools.partial(_na_kernel, variants=variants, nblk=nblk),
        grid=(heads, B),
        in_specs=[pl.BlockSpec((1, rpb_rows, LANES), lambda h, b: (h, 0, 0)), smem, smem,
                  tok(0), tok(heads), tok(2 * heads), tok(0)],
        out_specs=tok(0),
        out_shape=jax.ShapeDtypeStruct((B, S, heads * HEAD_DIM), BF16),
        scratch_shapes=[pltpu.VMEM((n_ct, 2 * NA_KH - 1, NA_QCOLS, LANES), F32),
                        pltpu.VMEM((len(variants), n_ct, NA_QROWS * NA_QCOLS, NA_KROWS * NA_KCOLS), F32)],
        compiler_params=_params("arbitrary", "arbitrary"),
        name="na_attention",
    )(rpb, jnp.asarray(kb0s, jnp.int32), jnp.asarray(vids, jnp.int32), qkv, qkv, qkv, gate)


DL_QB = 128
DL_UNROLL = 32


def _dilated_kernel(q_ref, k_ref, v_ref, o_ref, lse_ref, mask_ref, *stage_refs, dil, half, L):
    h = pl.program_id(1)
    qb = min(DL_QB, L - 2 * half)
    kb = qb + 2 * half
    nqb = L // qb
    d_in = min(dil, MAX_ROW_STRIDE)
    d_out = dil // d_in

    @pl.when(h == 0)
    def _():
        lse_ref[...] = jnp.zeros_like(lse_ref)

    @pl.when((pl.program_id(0) == 0) & (h == 0))
    def _():
        rel = (lax.broadcasted_iota(jnp.int32, (qb, kb), 1) - lax.broadcasted_iota(jnp.int32, (qb, kb), 0))
        for vi in range(3):
            mask_ref[vi] = jnp.where(jnp.abs(rel - vi * half) <= half, 0.0, NEG_INF)

    def load(r, q0):
        k0 = pl.multiple_of(jnp.clip(q0 - half, 0, L - kb), half)
        return (q_ref[0, r, pl.ds(q0, qb), :], k_ref[0, r, pl.ds(k0, kb), :], v_ref[0, r, pl.ds(k0, kb), :],
                mask_ref[(q0 - k0) // half])

    def put_lse(rows, lse):
        lane = lax.broadcasted_iota(jnp.int32, lse.shape, 1)
        lse_ref[0, rows, :] = jnp.where(lane == h, lse, lse_ref[0, rows, :])

    if d_out == 1:
        unroll = math.gcd(dil * nqb, DL_UNROLL)

        def group(t, carry):
            blocks = [((t * unroll + u) // nqb, pl.multiple_of(((t * unroll + u) % nqb) * qb, qb))
                      for u in range(unroll)]
            for (r, q0), (o, m, den) in zip(blocks, _attend([load(r, q0) for r, q0 in blocks])):
                rows = pl.ds(q0, qb) if dil == 1 else pl.ds(q0 * dil + r, qb, stride=dil)
                o_ref[0, rows, :] = o
                put_lse(rows, m * LN2 + jnp.log(den))
            return carry

        lax.fori_loop(0, dil * nqb // unroll, group, 0)
    else:
        o_tmp, l_tmp = stage_refs

        def group(r0, carry):
            blocks = [(qi, r1) for qi in range(nqb) for r1 in range(d_out)]
            outs = _attend([load(r1 * d_in + r0, qi * qb) for qi, r1 in blocks])
            for (qi, r1), (o, m, den) in zip(blocks, outs):
                hop1 = pl.ds(r1, qb, stride=d_out)
                o_tmp[qi, hop1, :] = o
                l_tmp[qi, hop1, :] = m * LN2 + jnp.log(den)
            for qi in range(nqb):
                hop2 = pl.ds(qi * qb * dil + r0, qb * d_out, stride=d_in)
                o_ref[0, hop2, :] = o_tmp[qi]
                put_lse(hop2, l_tmp[qi])
            return carry

        lax.fori_loop(0, d_in, group, 0)


def _dilated_attention(qk, v, *, B, dil, half, heads):
    L = qk.shape[2]
    S = L * dil
    qb = min(DL_QB, L - 2 * half)
    assert L % qb == 0 and heads <= LANES
    d_out = dil // min(dil, MAX_ROW_STRIDE)
    assert dil % min(dil, MAX_ROW_STRIDE) == 0
    scratch = [pltpu.VMEM((3, qb, qb + 2 * half), F32)]
    if d_out > 1:
        scratch += [pltpu.VMEM((L // qb, qb * d_out, LANES), F32)] * 2
    blk = lambda off: pl.BlockSpec((1, dil, L, HEAD_DIM), lambda b, h: (b, 0, 0, h + off))
    return pl.pallas_call(
        functools.partial(_dilated_kernel, dil=dil, half=half, L=L),
        grid=(B, heads),
        in_specs=[blk(0), blk(heads), blk(0)],
        out_specs=[pl.BlockSpec((1, S, HEAD_DIM), lambda b, h: (b, 0, h)),
                   pl.BlockSpec((1, S, LANES), lambda b, h: (b, 0, 0))],
        out_shape=[jax.ShapeDtypeStruct((B, S, heads * HEAD_DIM), F32),
                   jax.ShapeDtypeStruct((B, S, LANES), F32)],
        scratch_shapes=scratch,
        compiler_params=_params("arbitrary", "arbitrary"),
        name=f"dilated_attention_d{dil}",
    )(qk, qk, v)


def _mix_out_kernel(*refs, n_groups, heads):
    o_refs = refs[:n_groups]
    l_refs = refs[n_groups:2 * n_groups]
    g_ref, w_ref, x_ref, nw_ref, out_ref, ya_ref, yb_ref = refs[2 * n_groups:]
    s = pl.program_id(0)
    D = out_ref.shape[1]

    @pl.when(s == 0)
    def _():
        yb_ref[...] = jnp.zeros_like(yb_ref)

    def step(y_fill, y_use):
        lses = [r[...] for r in l_refs]
        mx = functools.reduce(jnp.maximum, lses)
        es = [jnp.exp(l - mx) for l in lses]
        inv = 1.0 / functools.reduce(lambda a, b: a + b, es)
        wts = [e * inv for e in es]
        for h in range(heads):
            sl = slice(h * HEAD_DIM, (h + 1) * HEAD_DIM)
            o = None
            for wt, o_ref in zip(wts, o_refs):
                t = wt[:, h:h + 1] * o_ref[:, sl]
                o = t if o is None else o + t
            g = g_ref[:, sl]
            y_fill[:, sl] = (o * _silu(g)).astype(y_fill.dtype)

        ssq = None
        for c in range(D // MXU_COLS):
            cols = slice(c * MXU_COLS, (c + 1) * MXU_COLS)
            x = x_ref[:, cols] + jnp.dot(y_use[...], w_ref[:, cols], preferred_element_type=F32)
            out_ref[:, cols] = x
            sq = x * x
            for hh in range(MXU_COLS // LANES):
                part = sq[:, hh * LANES:(hh + 1) * LANES]
                ssq = part if ssq is None else ssq + part
        scale = lax.rsqrt(jnp.sum(ssq, axis=-1, keepdims=True) * (1.0 / D) + RMS_EPS)
        out_ref[...] = out_ref[...] * scale * nw_ref[...]

    @pl.when(s % 2 == 0)
    def _():
        step(ya_ref, yb_ref)

    @pl.when(s % 2 == 1)
    def _():
        step(yb_ref, ya_ref)


def _mix_out(os_, lses, gate, w, x, norm_w, *, heads, tm):
    M, D = x.shape
    width = heads * HEAD_DIM
    n = len(os_)
    n_tiles = M // tm
    fill = lambda c: pl.BlockSpec((tm, c), lambda s: (jnp.minimum(s, n_tiles - 1), 0))
    use = lambda c: pl.BlockSpec((tm, c), lambda s: (jnp.maximum(s - 1, 0), 0))
    return pl.pallas_call(
        functools.partial(_mix_out_kernel, n_groups=n, heads=heads),
        grid=(n_tiles + 1,),
        in_specs=[fill(width)] * n + [fill(LANES)] * n + [
            fill(width),
            pl.BlockSpec((width, D), lambda s: (0, 0), pipeline_mode=pl.Buffered(1)),
            use(D),
            pl.BlockSpec((1, D), lambda s: (0, 0))],
        out_specs=use(D),
        out_shape=jax.ShapeDtypeStruct((M, D), F32),
        scratch_shapes=[pltpu.VMEM((tm, width), BF16), pltpu.VMEM((tm, width), BF16)],
        compiler_params=_params("arbitrary"),
        name="mix_out_norm",
    )(*os_, *lses, gate, w, x, norm_w.reshape(1, D))


def _rope_tables(S):
    half = ROPE_DIM // 2
    inv_freq = jnp.power(ROPE_THETA, -jnp.arange(half, dtype=F32) * (2.0 / ROPE_DIM))
    ang = jnp.arange(S).astype(F32)[:, None] * inv_freq[None, :]
    cos, sin = jnp.cos(ang), jnp.sin(ang)
    zeros = jnp.zeros((S, LANES - ROPE_DIM), F32)
    return (jnp.concatenate([cos, cos, 1.0 + zeros], axis=1),
            jnp.concatenate([-sin, sin, zeros], axis=1))


def _deinterleave(t, dil):
    S, C = t.shape
    return t.reshape(S // dil, dil, C).transpose(1, 0, 2).reshape(S, C)


def _forward(x, norm_w, final_norm_w, na_w_in, na_rpb, na_w_out, dl_w_in, dl_w_out, *,
             tm_norm=512, tm_norm1=256, tm=1024, tm_plain=1024, tn_qkv0=768, tn=512, tm_out=256):
    B, S, D = x.shape
    M = B * S
    assert norm_w.shape[0] == 2 and na_w_in.shape[0] == 1 and dl_w_in.shape[0] == 1
    na_width = na_w_out.shape[1]
    na_heads = na_width // HEAD_DIM
    dl_width = dl_w_out.shape[1]
    dl_heads = dl_width // HEAD_DIM
    n_groups = len(DL_GROUPS)

    (xn0,) = _rmsnorm(x, norm_w[0], (1,), tm_norm)
    xn0 = xn0.reshape(M, D)
    (qkv0,) = _proj(xn0, na_w_in[0], [_Seg(0, 3 * na_width, tn_qkv0, BF16, n_q_cols=na_width, rot=(GRID_W, NA_ROT))],
                    tm=tm_plain,
                    name="proj0_qkv")
    (gate0,) = _proj(xn0, na_w_in[0], [_Seg(3 * na_width, na_width, tn, F32)], tm=tm_plain, name="proj0_gate")
    y0 = _na_attention(qkv0.reshape(B, S, 3 * na_width), gate0.reshape(B, S, na_width), na_rpb[0],
                       heads=na_heads)
    (x1,) = _proj(y0.reshape(M, na_width), na_w_out[0], [_Seg(0, D, tn, F32, mode="res")], tm=tm_plain,
                  res=x.reshape(M, D), name="out0")

    w_in1 = dl_w_in[0]
    dils = tuple(sorted({d for _, d in DL_GROUPS}))
    xn1 = dict(zip(dils, _rmsnorm(x1.reshape(B, S, D), norm_w[1], dils, tm_norm1)))
    rope = _rope_tables(S)
    os_, lses = [], []
    for gi, (window, dil) in enumerate(DL_GROUPS):
        half = window // (2 * dil)
        off = gi * 3 * dl_width
        qk, v = _proj(xn1[dil].reshape(M, D), w_in1,
                      [_Seg(off, 2 * dl_width, tn, BF16, mode="rope", n_q_cols=dl_width),
                       _Seg(off + 2 * dl_width, dl_width, tn // 2, BF16)],
                      tm=tm, rope=[_deinterleave(t, dil) for t in rope], name=f"proj1_g{gi}")
        o, lse = _dilated_attention(qk.reshape(B, dil, S // dil, 2 * dl_width),
                                    v.reshape(B, dil, S // dil, dl_width), B=B, dil=dil, half=half,
                                    heads=dl_heads)
        os_.append(o.reshape(M, dl_width))
        lses.append(lse.reshape(M, LANES))
    (gate1,) = _proj(xn1[1].reshape(M, D), w_in1, [_Seg(3 * n_groups * dl_width, dl_width, tn, F32)], tm=tm_plain,
                     name="proj1_gate")
    out = _mix_out(os_, lses, gate1, dl_w_out[0].astype(BF16), x1, final_norm_w, heads=dl_heads, tm=tm_out)
    return out.reshape(B, S, D)


def kernel(x, norm_w, final_norm_w, na_w_in, na_rpb, na_w_out, dl_w_in, dl_w_out):
    return _forward(x, norm_w, final_norm_w, na_w_in, na_rpb, na_w_out, dl_w_in, dl_w_out)
```
